```python
import math
import jax
import jax.numpy as jnp
from jax import lax
import numpy as np

D_MODEL = 1024
BATCH = 8
SEQ = 2048
DEPTH = 4

CTX_LEN = 256
GRID_W = 64
N_MIXERS = 4
N_GQA = (DEPTH + 3) // N_MIXERS
N_S5 = (DEPTH + 2) // N_MIXERS
N_SSD = (DEPTH + 1) // N_MIXERS
N_NA = DEPTH // N_MIXERS
N_MOD = 6
D_FF = 4 * D_MODEL
NORM_EPS = 1e-6

GQA_HEAD_DIM = 128
GQA_HEADS = D_MODEL // GQA_HEAD_DIM
GQA_KV_HEADS = 2
GQA_GROUP = GQA_HEADS // GQA_KV_HEADS
Q_BLOCK = 128
ROPE_THETA = 10000.0

S5_GROUP = 16
S5_GROUPS = D_MODEL // S5_GROUP
S5_STATE = 64
S5_DT_MIN = 0.001
S5_DT_MAX = 0.1

SSD_D_INNER = 2 * D_MODEL
SSD_HEAD_DIM = 64
SSD_HEADS = SSD_D_INNER // SSD_HEAD_DIM
SSD_GROUPS = 4
SSD_STATE = 128
SSD_CONV = 5
SSD_CHUNK = 128
SSD_CONV_CH = SSD_D_INNER + 2 * SSD_GROUPS * SSD_STATE
SSD_IN = SSD_D_INNER + SSD_CONV_CH + 2 * SSD_HEADS
SSD_DT_MIN = 0.001
SSD_DT_MAX = 0.1

NA_HEAD_DIM = 64
NA_HEADS = D_MODEL // NA_HEAD_DIM
NA_WIN_ROWS = 8
NA_WIN_COLS = 16

kernel_name = "hybrid_interleaved_flow_trunk"


def _rms(x, g):
    xf = x.astype(jnp.float32)
    y = xf * lax.rsqrt(jnp.mean(xf * xf, axis=-1, keepdims=True) + NORM_EPS)
    return (y * g.astype(jnp.float32)).astype(x.dtype)


def _mlp(h, w1, w2):
    return jnp.square(jax.nn.relu(h @ w1)) @ w2


def _attend(q, k, v):
    s = jnp.einsum("bqkgd,bskd->bkgqs", q, k, preferred_element_type=jnp.float32)
    p = jax.nn.softmax(s * (q.shape[-1] ** -0.5), axis=-1).astype(v.dtype)
    return jnp.einsum("bkgqs,bskd->bqkgd", p, v)


def _axial_rope_tables(n_tok, head_dim):
    pos = jnp.arange(n_tok)
    row = (pos // GRID_W).astype(jnp.float32)
    col = (pos % GRID_W).astype(jnp.float32)
    half = head_dim // 2
    inv_freq = 1.0 / (ROPE_THETA ** (jnp.arange(0, half, 2, dtype=jnp.float32) / half))
    ang_r = row[:, None] * inv_freq
    ang_c = col[:, None] * inv_freq
    return jnp.cos(ang_r), jnp.sin(ang_r), jnp.cos(ang_c), jnp.sin(ang_c)


def _rotate(x, cos, sin):
    x1, x2 = jnp.split(x, 2, axis=-1)
    cos = cos[None, :, None, :]
    sin = sin[None, :, None, :]
    return jnp.concatenate([x1 * cos - x2 * sin, x1 * sin + x2 * cos], axis=-1)


def _apply_axial_rope(x, tables):
    cr, sr, cc, sc = tables
    xf = x.astype(jnp.float32)
    half = x.shape[-1] // 2
    out = jnp.concatenate([_rotate(xf[..., :half], cr, sr), _rotate(xf[..., half:], cc, sc)], axis=-1)
    return out.astype(x.dtype)


def _gqa_mixer(hc, h, w_qkv, q_norm, k_norm, w_o, need_ctx):
    n_q = GQA_HEADS * GQA_HEAD_DIM

    def q_of(z):
        q = (z @ w_qkv[:, :n_q]).reshape(z.shape[:2] + (GQA_KV_HEADS, GQA_GROUP, GQA_HEAD_DIM))
        return _rms(q, q_norm)

    def kv_of(z):
        kv = (z @ w_qkv[:, n_q:]).reshape(z.shape[:2] + (2, GQA_KV_HEADS, GQA_HEAD_DIM))
        return _rms(kv[:, :, 0], k_norm), kv[:, :, 1]

    bsz, n_lat, _ = h.shape
    tables = _axial_rope_tables(n_lat, GQA_HEAD_DIM)
    q = _apply_axial_rope(q_of(h).reshape(bsz, n_lat, GQA_HEADS, GQA_HEAD_DIM), tables)
    k, v = kv_of(h)
    k = _apply_axial_rope(k, tables)
    kc, vc = kv_of(hc)
    k_all = jnp.concatenate([kc, k], axis=1)
    v_all = jnp.concatenate([vc, v], axis=1)
    q_blocks = jnp.moveaxis(
        q.reshape(bsz, n_lat // Q_BLOCK, Q_BLOCK, GQA_KV_HEADS, GQA_GROUP, GQA_HEAD_DIM), 1, 0)
    o = lax.map(lambda qb: _attend(qb, k_all, v_all), q_blocks)
    y = jnp.moveaxis(o, 0, 1).reshape(bsz, n_lat, n_q) @ w_o
    yc = None
    if need_ctx:
        yc = _attend(q_of(hc), kc, vc).reshape(hc.shape[0], hc.shape[1], n_q) @ w_o
    return yc, y


def _lin_combine(e1, e2):
    a1, b1 = e1
    a2, b2 = e2
    return a1 * a2, a2 * b1 + b2


def _s5_discretise(a_re, a_im, log_dt, b_re, b_im):
    lam = lax.complex(a_re.astype(jnp.float32), a_im.astype(jnp.float32))
    a_bar = jnp.exp(lam * jnp.exp(log_dt.astype(jnp.float32))[:, None])
    b = lax.complex(b_re.astype(jnp.float32), b_im.astype(jnp.float32))
    b_bar = ((a_bar - 1.0) / lam)[..., None] * b
    return a_bar, b_bar


def _s5_states(u, a_bar, b_bar, s0, reverse):
    bu = jnp.einsum("blgc,gpc->blgp", u, b_bar)
    if reverse:
        bu = jnp.flip(bu, 1)
    if s0 is not None:
        bu = bu.at[:, 0].add(a_bar * s0)
    a_seq = jnp.broadcast_to(a_bar, (1, bu.shape[1]) + a_bar.shape)
    _, st = lax.associative_scan(_lin_combine, (a_seq, bu), axis=1)
    return jnp.flip(st, 1) if reverse else st


def _s5_readout(st, c_re, c_im):
    c = lax.complex(c_re.astype(jnp.float32), c_im.astype(jnp.float32))
    y = jnp.einsum("blgp,gcp->blgc", st, c).real
    return y.reshape(st.shape[0], st.shape[1], D_MODEL)


def _s5_mixer(hc, h, a_re, a_im, log_dt, b_re, b_im, c_re, c_im, d_skip, glu_w, glu_b, need_ctx):
    bsz, n_lat, _ = h.shape
    hf = h.astype(jnp.float32)
    hcf = hc.astype(jnp.float32)
    u = hf.reshape(bsz, n_lat, S5_GROUPS, S5_GROUP).astype(jnp.complex64)
    uc = hcf.reshape(bsz, hc.shape[1], S5_GROUPS, S5_GROUP).astype(jnp.complex64)
    d = d_skip.astype(jnp.float32)
    y = hf * d
    yc = hcf * d if need_ctx else None
    for direction, reverse in ((0, False), (1, True)):
        a_bar, b_bar = _s5_discretise(a_re[direction], a_im[direction], log_dt[direction],
                                      b_re[direction], b_im[direction])
        st_c = _s5_states(uc, a_bar, b_bar, None, reverse)
        s_end = st_c[:, 0] if reverse else st_c[:, -1]
        st = _s5_states(u, a_bar, b_bar, s_end, reverse)
        y = y + _s5_readout(st, c_re[direction], c_im[direction])
        if need_ctx:
            yc = yc + _s5_readout(st_c, c_re[direction], c_im[direction])

    def glu(z):
        g = jax.nn.gelu(z).astype(h.dtype)
        return g * jax.nn.sigmoid(g @ glu_w + glu_b)

    yc_out = glu(yc) if need_ctx else None
    return yc_out, glu(y)


def _dwconv_centred(x, w, b):
    ch = x.shape[-1]
    width = w.shape[0]
    y = lax.conv_general_dilated(
        x, w.astype(x.dtype)[:, None, :], window_strides=(1,),
        padding=[(width // 2, width // 2)], dimension_numbers=("NWC", "WIO", "NWC"),
        feature_group_count=ch)
    return y + b.astype(x.dtype)


def _segsum_exp(a_cum):
    t = a_cum.shape[-1]
    diff = a_cum[..., :, None] - a_cum[..., None, :]
    mask = jnp.tril(jnp.ones((t, t), dtype=bool))
    return jnp.exp(jnp.where(mask, diff, -jnp.inf))


def _ssd_chunked(xs, dt, a, bm, cm, h0, need_y):
    bsz, n, nh, hp = xs.shape
    ng, ns = bm.shape[-2:]
    nj = nh // ng
    nc = n // SSD_CHUNK
    xdt = (xs * dt[..., None]).reshape(bsz, nc, SSD_CHUNK, ng, nj, hp)
    a_cum = jnp.cumsum(
        jnp.moveaxis((dt * a).reshape(bsz, nc, SSD_CHUNK, ng, nj), (1, 2), (3, 4)), axis=-1)
    bm = bm.reshape(bsz, nc, SSD_CHUNK, ng, ns)
    cm = cm.reshape(bsz, nc, SSD_CHUNK, ng, ns)
    decay_states = jnp.exp(a_cum[..., -1:] - a_cum)
    states = jnp.einsum("bclgn,bgjcl,bclgjp->bcgjpn", bm, decay_states, xdt)
    states = jnp.concatenate([h0.reshape(bsz, 1, ng, nj, hp, ns), states], axis=1)
    chunk_cum = jnp.cumsum(jnp.pad(a_cum[..., -1], ((0, 0), (0, 0), (0, 0), (1, 0))), axis=-1)
    decay_chunk = _segsum_exp(chunk_cum)
    new_states = jnp.einsum("bgjzc,bcgjpn->bzgjpn", decay_chunk, states)
    h_final = new_states[:, -1].reshape(bsz, nh, hp, ns)
    if not need_y:
        return None, h_final
    lmat = _segsum_exp(a_cum)
    scores = jnp.einsum("bclgn,bcsgn->bgcls", cm, bm)
    y_diag = jnp.einsum("bgcls,bgjcls,bcsgjp->bclgjp", scores, lmat, xdt)
    y_off = jnp.einsum("bclgn,bcgjpn,bgjcl->bclgjp", cm, new_states[:, :-1], jnp.exp(a_cum))
    return (y_diag + y_off).reshape(bsz, n, nh, hp), h_final


def _flip_if(t, reverse):
    return jnp.flip(t, 1) if reverse else t


def _ssd_mixer(hc, h, w_in, conv_w, conv_b, dt_bias, a_log, d_skip, norm_g, w_out, need_ctx):
    nbc = SSD_GROUPS * SSD_STATE

    def prep(z, with_gate):
        bz, lz, _ = z.shape
        proj = z @ w_in[:, SSD_D_INNER:]
        xbc = jax.nn.silu(_dwconv_centred(proj[..., :SSD_CONV_CH], conv_w, conv_b)).astype(jnp.float32)
        xs = xbc[..., :SSD_D_INNER].reshape(bz, lz, SSD_HEADS, SSD_HEAD_DIM)
        bm = xbc[..., SSD_D_INNER:SSD_D_INNER + nbc].reshape(bz, lz, SSD_GROUPS, SSD_STATE)
        cm = xbc[..., SSD_D_INNER + nbc:].reshape(bz, lz, SSD_GROUPS, SSD_STATE)
        dt = jax.nn.softplus(proj[..., SSD_CONV_CH:].astype(jnp.float32).reshape(bz, lz, 2, SSD_HEADS)
                             + dt_bias.astype(jnp.float32))
        gate = z @ w_in[:, :SSD_D_INNER] if with_gate else None
        return gate, xs, bm, cm, dt

    gate, xs, bm, cm, dt = prep(h, True)
    gate_c, xs_c, bm_c, cm_c, dt_c = prep(hc, need_ctx)
    a = -jnp.exp(a_log.astype(jnp.float32))
    d = d_skip.astype(jnp.float32)[:, None]
    y = xs * d
    yc = xs_c * d if need_ctx else None
    h0 = jnp.zeros((h.shape[0], SSD_HEADS, SSD_HEAD_DIM, SSD_STATE), jnp.float32)
    for direction, reverse in ((0, False), (1, True)):
        y_c, s_c = _ssd_chunked(_flip_if(xs_c, reverse), _flip_if(dt_c[:, :, direction], reverse), a[direction],
                                _flip_if(bm_c, reverse), _flip_if(cm_c, reverse), h0, need_ctx)
        y_l, _ = _ssd_chunked(_flip_if(xs, reverse), _flip_if(dt[:, :, direction], reverse), a[direction],
                              _flip_if(bm, reverse), _flip_if(cm, reverse), s_c, True)
        y = y + _flip_if(y_l, reverse)
        if need_ctx:
            yc = yc + _flip_if(y_c, reverse)

    def out(yy, gt):
        bz, lz = yy.shape[:2]
        yg = yy.reshape(bz, lz, SSD_D_INNER) * jax.nn.silu(gt.astype(jnp.float32))
        yg = _rms(yg.reshape(bz, lz, SSD_GROUPS, SSD_D_INNER // SSD_GROUPS),
                  norm_g.reshape(SSD_GROUPS, SSD_D_INNER // SSD_GROUPS))
        return yg.reshape(bz, lz, SSD_D_INNER).astype(h.dtype) @ w_out

    yc_out = out(yc, gate_c) if need_ctx else None
    return yc_out, out(y, gate)


def _na_mixer(hc, h, w_qkv, rpb, w_o, need_ctx):
    bsz, n_lat, _ = h.shape
    rows = n_lat // GRID_W
    wr = min(NA_WIN_ROWS, rows)
    n_loc = wr * NA_WIN_COLS
    scale = NA_HEAD_DIM ** -0.5

    def q_of(z):
        return (z @ w_qkv[:, :D_MODEL]).reshape(z.shape[:2] + (NA_HEADS, NA_HEAD_DIM))

    def kv_of(z):
        kv = (z @ w_qkv[:, D_MODEL:]).reshape(z.shape[:2] + (2, NA_HEADS, NA_HEAD_DIM))
        return kv[:, :, 0], kv[:, :, 1]

    k, v = kv_of(h)
    kc, vc = kv_of(hc)
    k_grid = k.reshape(bsz, rows, GRID_W, NA_HEADS, NA_HEAD_DIM)
    v_grid = v.reshape(bsz, rows, GRID_W, NA_HEADS, NA_HEAD_DIM)
    q_rows = jnp.moveaxis(q_of(h).reshape(bsz, rows, GRID_W, NA_HEADS, NA_HEAD_DIM), 1, 0)
    cols = np.arange(GRID_W)
    col_start = np.clip(cols - NA_WIN_COLS // 2, 0, GRID_W - NA_WIN_COLS)
    col_idx = col_start[:, None] + np.arange(NA_WIN_COLS)[None, :]
    col_rel = col_idx - cols[:, None] + (NA_WIN_COLS - 1)

    def row_attend(args):
        r, q_row = args
        r0 = jnp.clip(r - wr // 2, 0, rows - wr)
        k_win = lax.dynamic_slice_in_dim(k_grid, r0, wr, axis=1)[:, :, col_idx]
        v_win = lax.dynamic_slice_in_dim(v_grid, r0, wr, axis=1)[:, :, col_idx]
        row_rel = r0 + jnp.arange(wr) - r + (NA_WIN_ROWS - 1)
        bias = rpb[:, row_rel[None, :, None], col_rel[:, None, :]]
        s_loc = jnp.einsum("bqhd,brqchd->bhqrc", q_row, k_win, preferred_element_type=jnp.float32) * scale
        s_loc = s_loc + bias.astype(jnp.float32)[None]
        s_ctx = jnp.einsum("bqhd,bshd->bhqs", q_row, kc, preferred_element_type=jnp.float32) * scale
        p = jax.nn.softmax(
            jnp.concatenate([s_loc.reshape(bsz, NA_HEADS, GRID_W, n_loc), s_ctx], axis=-1), axis=-1
        ).astype(v.dtype)
        p_loc = p[..., :n_loc].reshape(bsz, NA_HEADS, GRID_W, wr, NA_WIN_COLS)
        return (jnp.einsum("bhqrc,brqchd->bqhd", p_loc, v_win)
                + jnp.einsum("bhqs,bshd->bqhd", p[..., n_loc:], vc))

    o = lax.map(row_attend, (jnp.arange(rows), q_rows))
    y = jnp.moveaxis(o, 0, 1).reshape(bsz, n_lat, D_MODEL) @ w_o
    yc = None
    if need_ctx:
        yc = _attend(q_of(hc)[:, :, :, None], kc, vc).reshape(hc.shape[0], hc.shape[1], D_MODEL) @ w_o
    return yc, y


def setup_inputs(seed: int = 0) -> dict:
    key = jax.random.key(seed)
    keys = iter(jax.random.split(key, 48))
    f32 = jnp.float32

    def nrm(shape, scale=1.0):
        return jax.random.normal(next(keys), shape, f32) * scale

    def unif(shape, lo, hi):
        return jax.random.uniform(next(keys), shape, f32, lo, hi)

    d = D_MODEL
    s5_shape = (N_S5, 2, S5_GROUPS, S5_STATE)
    n_q = GQA_HEADS * GQA_HEAD_DIM
    ssd_dt = jnp.exp(unif((N_SSD, 2, SSD_HEADS), math.log(SSD_DT_MIN), math.log(SSD_DT_MAX)))
    return {
        "x": nrm((BATCH, SEQ, d)),
        "c": nrm((BATCH, d)),
        "ctx": nrm((BATCH, CTX_LEN, d)),
        "c_ctx": nrm((d,)),
        "ada_w": nrm((DEPTH, d, N_MOD * d), d ** -0.5),
        "ada_b": nrm((DEPTH, N_MOD * d), 0.02),
        "norm_g": 1.0 + nrm((DEPTH, 4, d), 0.05),
        "mlp_w1": nrm((DEPTH, d, D_FF), d ** -0.5),
        "mlp_w2": nrm((DEPTH, D_FF, d), D_FF ** -0.5),
        "gqa_w_qkv": nrm((N_GQA, d, n_q + 2 * GQA_KV_HEADS * GQA_HEAD_DIM), d ** -0.5),
        "gqa_q_norm": 1.0 + nrm((N_GQA, GQA_HEAD_DIM), 0.05),
        "gqa_k_norm": 1.0 + nrm((N_GQA, GQA_HEAD_DIM), 0.05),
        "gqa_w_o": nrm((N_GQA, n_q, d), n_q ** -0.5),
        "s5_a_re": -0.5 * jnp.exp(nrm(s5_shape, 0.02)),
        "s5_a_im": jnp.broadcast_to(jnp.pi * jnp.arange(S5_STATE, dtype=f32), s5_shape),
        "s5_log_dt": unif((N_S5, 2, S5_GROUPS), math.log(S5_DT_MIN), math.log(S5_DT_MAX)),
        "s5_b_re": nrm((N_S5, 2, S5_GROUPS, S5_STATE, S5_GROUP), (2 * S5_GROUP) ** -0.5),
        "s5_b_im": nrm((N_S5, 2, S5_GROUPS, S5_STATE, S5_GROUP), (2 * S5_GROUP) ** -0.5),
        "s5_c_re": nrm((N_S5, 2, S5_GROUPS, S5_GROUP, S5_STATE), (2 * S5_STATE) ** -0.5),
        "s5_c_im": nrm((N_S5, 2, S5_GROUPS, S5_GROUP, S5_STATE), (2 * S5_STATE) ** -0.5),
        "s5_d": nrm((N_S5, d)),
        "s5_glu_w": nrm((N_S5, d, d), d ** -0.5),
        "s5_glu_b": nrm((N_S5, d), 0.02),
        "ssd_w_in": nrm((N_SSD, d, SSD_IN), d ** -0.5),
        "ssd_conv_w": nrm((N_SSD, SSD_CONV, SSD_CONV_CH), SSD_CONV ** -0.5),
        "ssd_conv_b": nrm((N_SSD, SSD_CONV_CH), 0.02),
        "ssd_dt_bias": ssd_dt + jnp.log(-jnp.expm1(-ssd_dt)),
        "ssd_a_log": jnp.log(unif((N_SSD, 2, SSD_HEADS), 1.0, 16.0)),
        "ssd_d": 1.0 + nrm((N_SSD, SSD_HEADS), 0.05),
        "ssd_norm_g": 1.0 + nrm((N_SSD, SSD_D_INNER), 0.05),
        "ssd_w_out": nrm((N_SSD, SSD_D_INNER, d), SSD_D_INNER ** -0.5),
        "na_w_qkv": nrm((N_NA, d, 3 * d), d ** -0.5),
        "na_rpb": nrm((N_NA, NA_HEADS, 2 * NA_WIN_ROWS - 1, 2 * NA_WIN_COLS - 1), 0.02),
        "na_w_o": nrm((N_NA, d, d), d ** -0.5),
    }


def reference(x, c, ctx, c_ctx, ada_w, ada_b, norm_g, mlp_w1, mlp_w2,
              gqa_w_qkv, gqa_q_norm, gqa_k_norm, gqa_w_o,
              s5_a_re, s5_a_im, s5_log_dt, s5_b_re, s5_b_im, s5_c_re, s5_c_im, s5_d, s5_glu_w, s5_glu_b,
              ssd_w_in, ssd_conv_w, ssd_conv_b, ssd_dt_bias, ssd_a_log, ssd_d, ssd_norm_g, ssd_w_out,
              na_w_qkv, na_rpb, na_w_o):
    bsz = x.shape[0]
    xc = ctx
    silu_c = jax.nn.silu(c)
    silu_cc = jax.nn.silu(c_ctx)
    for i in range(DEPTH):
        kind = i % N_MIXERS
        j = i // N_MIXERS
        need_ctx = i < DEPTH - 1
        mod = (silu_c @ ada_w[i] + ada_b[i]).reshape(bsz, N_MOD, 1, D_MODEL)
        modc = (silu_cc @ ada_w[i] + ada_b[i]).reshape(N_MOD, D_MODEL)
        g = norm_g[i]
        h = _rms(x, g[0]) * (1.0 + mod[:, 1]) + mod[:, 0]
        hc = _rms(xc, g[0]) * (1.0 + modc[1]) + modc[0]
        if kind == 0:
            yc, y = _gqa_mixer(hc, h, gqa_w_qkv[j], gqa_q_norm[j], gqa_k_norm[j], gqa_w_o[j], need_ctx)
        elif kind == 1:
            yc, y = _s5_mixer(hc, h, s5_a_re[j], s5_a_im[j], s5_log_dt[j], s5_b_re[j], s5_b_im[j],
                              s5_c_re[j], s5_c_im[j], s5_d[j], s5_glu_w[j], s5_glu_b[j], need_ctx)
        elif kind == 2:
            yc, y = _ssd_mixer(hc, h, ssd_w_in[j], ssd_conv_w[j], ssd_conv_b[j], ssd_dt_bias[j],
                               ssd_a_log[j], ssd_d[j], ssd_norm_g[j], ssd_w_out[j], need_ctx)
        else:
            yc, y = _na_mixer(hc, h, na_w_qkv[j], na_rpb[j], na_w_o[j], need_ctx)
        x = x + mod[:, 2] * _rms(y, g[1])
        h = _rms(x, g[2]) * (1.0 + mod[:, 4]) + mod[:, 3]
        x = x + mod[:, 5] * _rms(_mlp(h, mlp_w1[i], mlp_w2[i]), g[3])
        if need_ctx:
            xc = xc + modc[2] * _rms(yc, g[1])
            hc = _rms(xc, g[2]) * (1.0 + modc[4]) + modc[3]
            xc = xc + modc[5] * _rms(_mlp(hc, mlp_w1[i], mlp_w2[i]), g[3])
    return x
```

```python
import functools

import numpy as np
import jax
import jax.numpy as jnp
from jax import lax
from jax.experimental import pallas as pl
from jax.experimental.pallas import tpu as pltpu

F32 = jnp.float32
BF16 = jnp.bfloat16

NORM_EPS = 1e-6
GRID_W = 64
ROW_TILE = 256
VMEM_LIMIT_BYTES = 56 * 1024 * 1024
NEG_BIG = -1e30

GQA_HEAD_DIM = 128
GQA_KV_HEADS = 2
ROPE_THETA = 10000.0
S5_GROUP = 16
S5_STATE = 64
S5_CHUNK = 16
SSD_HEAD_DIM = 64
SSD_GROUPS = 4
SSD_STATE = 128
SSD_CHUNK = 128
NA_HEAD_DIM = 64
NA_WIN_ROWS = 8
NA_WIN_COLS = 16


def _params(*semantics):
    return pltpu.CompilerParams(dimension_semantics=semantics, vmem_limit_bytes=VMEM_LIMIT_BYTES)


def _rms(y, g):
    return y * lax.rsqrt(jnp.mean(y * y, axis=-1, keepdims=True) + NORM_EPS) * g


def _dot(a, b):
    return jnp.dot(a, b, preferred_element_type=F32)


def _dot_nt(a, b):
    return lax.dot_general(a, b, (((1,), (1,)), ((), ())), preferred_element_type=F32)


def _dot_tn(a, b):
    return lax.dot_general(a, b, (((0,), (0,)), ((), ())), preferred_element_type=F32)


def _sigmoid(z):
    return 1.0 / (1.0 + jnp.exp(-z))


def _const_spec(shape):
    nd = len(shape)
    return pl.BlockSpec(shape, lambda *_: (0,) * nd)


def _mod_spec(d, ctx_tiles):
    return pl.BlockSpec((1, 1, 8, d), lambda b, t: (b, jnp.where(t < ctx_tiles, 0, 1), 0, 0))


def _row_spec(n, tm=ROW_TILE):
    return pl.BlockSpec((1, tm, n), lambda b, t: (b, t, 0))


def _ada_kernel(c_ref, w_ref, b_ref, o_ref):
    cc = c_ref[...]
    s = cc * _sigmoid(cc)
    o_ref[0] = _dot(s.astype(BF16), w_ref[0].astype(BF16)) + b_ref[0]


def _ada(cond, ada_w, ada_b):
    depth, d, n = ada_w.shape
    rows = cond.shape[0]
    tn = 1536
    return pl.pallas_call(
        _ada_kernel,
        grid=(depth, n // tn),
        in_specs=[
            pl.BlockSpec((rows, d), lambda i, j: (0, 0)),
            pl.BlockSpec((1, d, tn), lambda i, j: (i, 0, j)),
            pl.BlockSpec((1, 1, tn), lambda i, j: (i, 0, j)),
        ],
        out_specs=pl.BlockSpec((1, rows, tn), lambda i, j: (i, 0, j)),
        out_shape=jax.ShapeDtypeStruct((depth, rows, n), F32),
        compiler_params=_params("parallel", "parallel"),
        name="ada",
    )(cond, ada_w, ada_b.reshape(depth, 1, n))


def _nm_matmul_kernel(x_ref, md_ref, g_ref, w_ref, *o_refs, splits):
    md = md_ref[0, 0]
    h = _rms(x_ref[0], g_ref[0:1]) * (1.0 + md[1:2]) + md[0:1]
    hb = h.astype(BF16)
    for o_ref, (lo, hi) in zip(o_refs, splits):
        o_ref[0] = _dot(hb, w_ref[:, lo:hi]).astype(o_ref.dtype)


def _nm_matmul(xs, md, g, w, splits, dtypes, ctx_tiles=1):
    b, s, d = xs.shape
    out_shape = [jax.ShapeDtypeStruct((b, s, hi - lo), dt) for (lo, hi), dt in zip(splits, dtypes)]
    return pl.pallas_call(
        functools.partial(_nm_matmul_kernel, splits=splits),
        grid=(b, s // ROW_TILE),
        in_specs=[_row_spec(d), _mod_spec(d, ctx_tiles), _const_spec(g.shape), _const_spec(w.shape)],
        out_specs=[_row_spec(hi - lo) for lo, hi in splits],
        out_shape=out_shape,
        compiler_params=_params("parallel", "parallel"),
        name="nm_matmul",
    )(xs, md, g, w)


def _nm_kernel(x_ref, md_ref, g_ref, o_ref):
    md = md_ref[0, 0]
    h = _rms(x_ref[0], g_ref[0:1]) * (1.0 + md[1:2]) + md[0:1]
    o_ref[0] = h.astype(o_ref.dtype)


def _nm(xs, md, g, dtype, ctx_tiles=1):
    b, s, d = xs.shape
    return pl.pallas_call(
        _nm_kernel,
        grid=(b, s // ROW_TILE),
        in_specs=[_row_spec(d), _mod_spec(d, ctx_tiles), _const_spec(g.shape)],
        out_specs=_row_spec(d),
        out_shape=jax.ShapeDtypeStruct((b, s, d), dtype),
        compiler_params=_params("parallel", "parallel"),
        name="nm",
    )(xs, md, g)


def _mlp_kernel(x_ref, md_ref, g_ref, w1_ref, w2_ref, o_ref, *, ff_chunk):
    x = x_ref[0]
    md = md_ref[0, 0]
    hb = (_rms(x, g_ref[2:3]) * (1.0 + md[4:5]) + md[3:4]).astype(BF16)
    acc = jnp.zeros(x.shape, F32)
    for lo in range(0, w1_ref.shape[1], ff_chunk):
        a = _dot(hb, w1_ref[:, lo:lo + ff_chunk])
        a = jnp.square(jnp.maximum(a, 0.0)).astype(BF16)
        acc = acc + _dot(a, w2_ref[lo:lo + ff_chunk, :])
    o_ref[0] = x + md[5:6] * _rms(acc, g_ref[3:4])


def _mlp(xs, md, g, w1, w2, ctx_tiles=1):
    b, s, d = xs.shape
    return pl.pallas_call(
        functools.partial(_mlp_kernel, ff_chunk=1024),
        grid=(b, s // ROW_TILE),
        in_specs=[_row_spec(d), _mod_spec(d, ctx_tiles), _const_spec(g.shape),
                  _const_spec(w1.shape), _const_spec(w2.shape)],
        out_specs=_row_spec(d),
        out_shape=jax.ShapeDtypeStruct((b, s, d), F32),
        compiler_params=_params("parallel", "parallel"),
        name="mlp",
    )(xs, md, g, w1, w2)


def _proj_res_kernel(y_ref, x_ref, md_ref, g_ref, w_ref, o_ref):
    r = _dot(y_ref[0].astype(BF16), w_ref[...])
    o_ref[0] = x_ref[0] + md_ref[0, 0][2:3] * _rms(r, g_ref[1:2])


def _proj_res(y, xs, md, g, w, ctx_tiles=1):
    b, s, d = xs.shape
    return pl.pallas_call(
        _proj_res_kernel,
        grid=(b, s // ROW_TILE),
        in_specs=[_row_spec(y.shape[-1]), _row_spec(d), _mod_spec(d, ctx_tiles), _const_spec(g.shape),
                  _const_spec(w.shape)],
        out_specs=_row_spec(d),
        out_shape=jax.ShapeDtypeStruct((b, s, d), F32),
        compiler_params=_params("parallel", "parallel"),
        name="proj_res",
    )(y, xs, md, g, w)


def _rope(x, cos, sin_signed):
    lane = lax.broadcasted_iota(jnp.int32, x.shape, 1)
    quarter = GQA_HEAD_DIM // 4
    partner = jnp.where((lane % (2 * quarter)) < quarter,
                        pltpu.roll(x, GQA_HEAD_DIM - quarter, 1), pltpu.roll(x, quarter, 1))
    return x * cos + partner * sin_signed


def _gqa_kernel(q_ref, k_ref, v_ref, cq_ref, sq_ref, ck_ref, sk_ref, qn_ref, kn_ref, o_ref, kb_ref, vb_ref,
                *, ctx_len, group):
    t = pl.program_id(2)
    tq = q_ref.shape[1]
    s_all = k_ref.shape[1]
    hd = GQA_HEAD_DIM

    @pl.when(t == 0)
    def _():
        kn = _rms(k_ref[0], kn_ref[...])
        kb_ref[...] = _rope(kn, ck_ref[...], sk_ref[...]).astype(BF16)
        vb_ref[...] = v_ref[0].astype(BF16)

    kb = kb_ref[...]
    vb = vb_ref[...]
    cos = cq_ref[...]
    sin = sq_ref[...]
    kpos = lax.broadcasted_iota(jnp.int32, (tq, s_all), 1)
    visible = kpos < jnp.where(t * tq < ctx_len, ctx_len, s_all)
    scale = hd ** -0.5
    for g in range(group):
        q = q_ref[0, :, g * hd:(g + 1) * hd]
        qb = (_rope(_rms(q, qn_ref[...]), cos, sin) * scale).astype(BF16)
        s = jnp.where(visible, _dot_nt(qb, kb), NEG_BIG)
        p = jnp.exp(s - jnp.max(s, axis=-1, keepdims=True))
        denom = jnp.sum(p, axis=-1, keepdims=True)
        o_ref[0, :, g * hd:(g + 1) * hd] = (_dot(p.astype(BF16), vb) / denom).astype(o_ref.dtype)


def _rope_tables(n_lat, ctx_len):
    pos = np.arange(n_lat)
    row = (pos // GRID_W).astype(np.float32)
    col = (pos % GRID_W).astype(np.float32)
    half = GQA_HEAD_DIM // 2
    inv_freq = (1.0 / (ROPE_THETA ** (np.arange(0, half, 2, dtype=np.float32) / half))).astype(np.float32)
    ar = row[:, None] * inv_freq
    ac = col[:, None] * inv_freq
    cos = np.concatenate([np.cos(ar), np.cos(ar), np.cos(ac), np.cos(ac)], axis=1)
    sin = np.concatenate([-np.sin(ar), np.sin(ar), -np.sin(ac), np.sin(ac)], axis=1)
    cos = np.concatenate([np.ones((ctx_len, GQA_HEAD_DIM)), cos], axis=0).astype(np.float32)
    sin = np.concatenate([np.zeros((ctx_len, GQA_HEAD_DIM)), sin], axis=0).astype(np.float32)
    return jnp.asarray(cos), jnp.asarray(sin)


def _gqa_attention(qkv, q_norm, k_norm, ctx_len):
    b, s, n = qkv.shape
    hd = GQA_HEAD_DIM
    n_q = n - 2 * GQA_KV_HEADS * hd
    group = n_q // hd // GQA_KV_HEADS
    cos, sin = _rope_tables(s - ctx_len, ctx_len)
    tq = ROW_TILE
    q_blocks = n_q // hd
    return pl.pallas_call(
        functools.partial(_gqa_kernel, ctx_len=ctx_len, group=group),
        grid=(b, GQA_KV_HEADS, s // tq),
        in_specs=[
            pl.BlockSpec((1, tq, group * hd), lambda i, h, t: (i, t, h)),
            pl.BlockSpec((1, s, hd), lambda i, h, t: (i, 0, q_blocks + h)),
            pl.BlockSpec((1, s, hd), lambda i, h, t: (i, 0, q_blocks + GQA_KV_HEADS + h)),
            pl.BlockSpec((tq, hd), lambda i, h, t: (t, 0)),
            pl.BlockSpec((tq, hd), lambda i, h, t: (t, 0)),
            _const_spec((s, hd)),
            _const_spec((s, hd)),
            _const_spec((1, hd)),
            _const_spec((1, hd)),
        ],
        out_specs=pl.BlockSpec((1, tq, group * hd), lambda i, h, t: (i, t, h)),
        out_shape=jax.ShapeDtypeStruct((b, s, n_q), BF16),
        scratch_shapes=[pltpu.VMEM((s, hd), BF16), pltpu.VMEM((s, hd), BF16)],
        compiler_params=_params("parallel", "parallel", "arbitrary"),
        name="gqa_attention",
    )(qkv, qkv, qkv, cos, sin, cos, sin, q_norm.reshape(1, hd), k_norm.reshape(1, hd))


def _s5_kernel(u_ref, m_ref, p_ref, q_ref, a_ref, y_ref, s_ref, h_ref, *, n_chunks, ctx_chunks, nb):
    half = S5_STATE
    u = u_ref[0]
    s_ref[...] = _dot(u, p_ref[0])
    a_re = a_ref[0, 0:1]
    a_im = a_ref[0, 1:2]
    is_fwd = lax.broadcasted_iota(jnp.int32, (nb, 2 * half), 1) < half

    def step(t, carry):
        h_re, h_im = carry
        k_rev = jnp.where(t < ctx_chunks, ctx_chunks - 1 - t, n_chunks - 1 - (t - ctx_chunks))
        rf = pl.multiple_of(t * nb, nb)
        rr = pl.multiple_of(k_rev * nb, nb)
        h_ref[pl.ds(rf, nb), 0:half] = h_re[:, 0:half]
        h_ref[pl.ds(rr, nb), half:2 * half] = h_re[:, half:2 * half]
        h_ref[pl.ds(rf, nb), 2 * half:3 * half] = h_im[:, 0:half]
        h_ref[pl.ds(rr, nb), 3 * half:4 * half] = h_im[:, half:2 * half]
        s_re = jnp.where(is_fwd, s_ref[pl.ds(rf, nb), 0:2 * half], s_ref[pl.ds(rr, nb), 0:2 * half])
        s_im = jnp.where(is_fwd, s_ref[pl.ds(rf, nb), 2 * half:4 * half], s_ref[pl.ds(rr, nb), 2 * half:4 * half])
        return a_re * h_re - a_im * h_im + s_re, a_re * h_im + a_im * h_re + s_im

    zero = jnp.zeros((nb, 2 * half), F32)
    lax.fori_loop(0, n_chunks, step, (zero, zero))
    y_ref[0] = _dot(u, m_ref[0]) + _dot(h_ref[...].astype(BF16), q_ref[0])


def _s5_tables(a_re, a_im, log_dt, b_re, b_im, c_re, c_im, d_skip):
    tc = S5_CHUNK
    n_groups = a_re.shape[1]
    lam = lax.complex(a_re.astype(F32), a_im.astype(F32))
    dt = jnp.exp(log_dt.astype(F32))[..., None]
    a_bar = jnp.exp(lam * dt)
    b_bar = ((a_bar - 1.0) / lam)[..., None] * lax.complex(b_re.astype(F32), b_im.astype(F32))
    cc = lax.complex(c_re.astype(F32), c_im.astype(F32))
    steps = jnp.arange(tc + 1, dtype=F32)
    apow = jnp.exp((lam * dt)[None] * steps[:, None, None, None])
    kern = jnp.einsum("dgcp,ldgp,dgpe->dlgce", cc, apow[:tc], b_bar).real
    jj = np.arange(tc)[:, None]
    ii = np.arange(tc)[None, :]
    kf = kern[0][np.clip(ii - jj, 0, tc - 1)] * jnp.asarray(ii >= jj, F32)[:, :, None, None, None]
    kr = kern[1][np.clip(jj - ii, 0, tc - 1)] * jnp.asarray(jj >= ii, F32)[:, :, None, None, None]
    m = jnp.transpose(kf + kr, (2, 0, 4, 1, 3))
    skip = d_skip.astype(F32).reshape(n_groups, S5_GROUP)
    eye_t = jnp.eye(tc, dtype=F32)
    eye_c = jnp.eye(S5_GROUP, dtype=F32)
    m = m + skip[:, None, :, None, None] * eye_t[None, :, None, :, None] * eye_c[None, None, :, None, :]
    m = m.reshape(n_groups, tc * S5_GROUP, tc * S5_GROUP)
    pf = jnp.einsum("jgp,gpe->gjep", apow[:tc][::-1, 0], b_bar[0])
    pr = jnp.einsum("jgp,gpe->gjep", apow[:tc, 1], b_bar[1])
    p = jnp.concatenate([pf.real, pr.real, pf.imag, pr.imag], axis=-1).reshape(n_groups, tc * S5_GROUP, 4 * S5_STATE)
    wf = jnp.einsum("gcp,igp->gpic", cc[0], apow[1:, 0])
    wr = jnp.einsum("gcp,igp->gpic", cc[1], apow[1:][::-1, 1])
    q = jnp.concatenate([wf.real, wr.real, -wf.imag, -wr.imag], axis=1).reshape(n_groups, 4 * S5_STATE, tc * S5_GROUP)
    a_tc = apow[tc]
    dec = jnp.stack([jnp.concatenate([a_tc[0].real, a_tc[1].real], axis=-1),
                     jnp.concatenate([a_tc[0].imag, a_tc[1].imag], axis=-1)], axis=1)
    dec = jnp.pad(dec, ((0, 0), (0, 6), (0, 0)))
    return m.astype(BF16), p.astype(BF16), q.astype(BF16), dec


def _s5_core(h, tables, ctx_len):
    b, s, d = h.shape
    tc = S5_CHUNK
    n_groups = d // S5_GROUP
    n_chunks = s // tc
    w = tc * S5_GROUP
    m, p, q, dec = tables
    u = h.reshape(b, n_chunks, tc, n_groups, S5_GROUP).transpose(3, 1, 0, 2, 4).reshape(n_groups, n_chunks * b, w)
    grp = lambda g: (g, 0, 0)
    y = pl.pallas_call(
        functools.partial(_s5_kernel, n_chunks=n_chunks, ctx_chunks=ctx_len // tc, nb=b),
        grid=(n_groups,),
        in_specs=[
            pl.BlockSpec((1, n_chunks * b, w), grp),
            pl.BlockSpec((1, w, w), grp),
            pl.BlockSpec((1, w, 4 * S5_STATE), grp),
            pl.BlockSpec((1, 4 * S5_STATE, w), grp),
            pl.BlockSpec((1, 8, 2 * S5_STATE), grp),
        ],
        out_specs=pl.BlockSpec((1, n_chunks * b, w), grp),
        out_shape=jax.ShapeDtypeStruct((n_groups, n_chunks * b, w), F32),
        scratch_shapes=[pltpu.VMEM((n_chunks * b, 4 * S5_STATE), F32), pltpu.VMEM((n_chunks * b, 4 * S5_STATE), F32)],
        compiler_params=_params("parallel"),
        name="s5_core",
    )(u, m, p, q, dec)
    return y.reshape(n_groups, n_chunks, b, tc, S5_GROUP).transpose(2, 1, 3, 0, 4).reshape(b, s, d)


def _glu_res_kernel(y_ref, x_ref, md_ref, g_ref, w_ref, b_ref, o_ref):
    y = y_ref[0]
    gel = y * (0.5 * (1.0 + jnp.tanh(np.sqrt(2.0 / np.pi).astype(np.float32) * (y + 0.044715 * (y * y * y)))))
    out = gel * _sigmoid(_dot(gel.astype(BF16), w_ref[...]) + b_ref[...])
    o_ref[0] = x_ref[0] + md_ref[0, 0][2:3] * _rms(out, g_ref[1:2])


def _glu_res(y, xs, md, g, w, bias, ctx_tiles=1):
    b, s, d = xs.shape
    return pl.pallas_call(
        _glu_res_kernel,
        grid=(b, s // ROW_TILE),
        in_specs=[_row_spec(d), _row_spec(d), _mod_spec(d, ctx_tiles), _const_spec(g.shape),
                  _const_spec(w.shape), _const_spec((1, d))],
        out_specs=_row_spec(d),
        out_shape=jax.ShapeDtypeStruct((b, s, d), F32),
        compiler_params=_params("parallel", "parallel"),
        name="glu_res",
    )(y, xs, md, g, w, bias.reshape(1, d))


def _conv_silu_kernel(x_ref, prev_ref, next_ref, w_ref, b_ref, o_ref, *, ctx_len, seq_len, width):
    t = pl.program_id(1)
    tm = x_ref.shape[1]
    x = x_ref[0]
    lo = t * tm
    hi = lo + tm
    prev_ok = jnp.where((lo == 0) | (lo == ctx_len), 0.0, 1.0)
    next_ok = jnp.where((hi == ctx_len) | (hi == seq_len), 0.0, 1.0)
    prev = prev_ref[0] * prev_ok
    nxt = next_ref[0] * next_ok
    row = lax.broadcasted_iota(jnp.int32, x.shape, 0)
    half = width // 2
    acc = x * w_ref[half:half + 1] + b_ref[...]
    for k in range(1, half + 1):
        back = pltpu.roll(x, k, 0)
        for r in range(k):
            back = jnp.where(row == r, prev[8 - k + r:8 - k + r + 1], back)
        acc = acc + back * w_ref[half - k:half - k + 1]
        fwd = pltpu.roll(x, tm - k, 0)
        for r in range(k):
            fwd = jnp.where(row == tm - k + r, nxt[r:r + 1], fwd)
        acc = acc + fwd * w_ref[half + k:half + k + 1]
    o_ref[0] = acc * _sigmoid(acc)


def _conv_silu(x, w, bias, ctx_len, width):
    b, s, n = x.shape
    tm = ROW_TILE
    sub = tm // 8
    last = s // 8 - 1
    return pl.pallas_call(
        functools.partial(_conv_silu_kernel, ctx_len=ctx_len, seq_len=s, width=width),
        grid=(b, s // tm),
        in_specs=[
            _row_spec(n),
            pl.BlockSpec((1, 8, n), lambda i, t: (i, jnp.maximum(t * sub - 1, 0), 0)),
            pl.BlockSpec((1, 8, n), lambda i, t: (i, jnp.minimum((t + 1) * sub, last), 0)),
            _const_spec(w.shape),
            _const_spec((1, n)),
        ],
        out_specs=_row_spec(n),
        out_shape=jax.ShapeDtypeStruct((b, s, n), F32),
        compiler_params=_params("parallel", "parallel"),
        name="conv_silu",
    )(x, x, x, w, bias.reshape(1, n))


def _softplus(z):
    return jnp.maximum(z, 0.0) + jnp.log1p(jnp.exp(-jnp.abs(z)))


def _expand_heads(v, e3):
    hi = v.astype(BF16)
    r1 = v - hi.astype(F32)
    mid = r1.astype(BF16)
    lo = (r1 - mid.astype(F32)).astype(BF16)
    return _dot(jnp.concatenate([hi, mid, lo], axis=1), e3)


def _ssd_kernel(*refs, rev, n_heads):
    if rev:
        (xs_ref, bm_ref, cm_ref, dtc_ref, dtr_ref, pr_ref, pc_ref, yprev_ref, dsk_ref, y_ref, st_ref) = refs
    else:
        (xs_ref, bm_ref, cm_ref, dtc_ref, dtr_ref, pr_ref, pc_ref, y_ref, st_ref) = refs
    t = pl.program_id(1)
    tc = xs_ref.shape[1]
    hd = SSD_HEAD_DIM
    pair = 2 * hd
    heads_per_group = n_heads // SSD_GROUPS

    @pl.when(t == 0)
    def _():
        st_ref[...] = jnp.zeros(st_ref.shape, F32)

    dtc = _softplus(dtc_ref[0, 0] + pr_ref[0:1])
    dtr = _softplus(dtr_ref[0, 0] + pc_ref[:, 0:1])
    dac = dtc * -jnp.exp(pr_ref[1:2])
    dar = dtr * -jnp.exp(pc_ref[:, 1:2])
    ii = lax.broadcasted_iota(jnp.int32, (tc, tc), 0)
    jj = lax.broadcasted_iota(jnp.int32, (tc, tc), 1)
    visible = (jj >= ii) if rev else (ii >= jj)
    tri = jnp.where(visible, 1.0, 0.0)
    hp = lax.Precision.HIGHEST
    cum_c = jnp.dot(tri, dac, precision=hp, preferred_element_type=F32)
    cum_r = lax.dot_general(dar, tri, (((1,), (1,)), ((), ())), precision=hp, preferred_element_type=F32)
    tot = cum_c[0:1] if rev else cum_c[tc - 1:tc]

    e_row = lax.broadcasted_iota(jnp.int32, (3 * n_heads, n_heads * hd), 0)
    e_col = lax.broadcasted_iota(jnp.int32, (3 * n_heads, n_heads * hd), 1)
    e3 = jnp.where((e_row % n_heads) == (e_col // hd), 1.0, 0.0).astype(BF16)
    w_out = _expand_heads(dtc * jnp.exp(tot - cum_c), e3)
    w_in = _expand_heads(jnp.exp(cum_c), e3)
    dec = _expand_heads(jnp.broadcast_to(jnp.exp(tot), (8, n_heads)), e3)[0:1]

    xs = xs_ref[0]
    lane = lax.broadcasted_iota(jnp.int32, (tc, pair), 1)
    first = lane < hd
    for g in range(SSD_GROUPS):
        bmb = bm_ref[0, :, g * SSD_STATE:(g + 1) * SSD_STATE].astype(BF16)
        cmb = cm_ref[0, :, g * SSD_STATE:(g + 1) * SSD_STATE].astype(BF16)
        scores = _dot_nt(cmb, bmb)
        for k in range(heads_per_group // 2):
            hp_i = g * (heads_per_group // 2) + k
            cols = slice(hp_i * pair, (hp_i + 1) * pair)
            x2 = xs[:, cols]
            x2b = x2.astype(BF16)
            halves = []
            for s_i in range(2):
                h = 2 * hp_i + s_i
                decay = jnp.exp(jnp.where(visible, cum_c[:, h:h + 1] - cum_r[h:h + 1, :], -jnp.inf))
                wgt = (scores * decay * dtr[h:h + 1, :]).astype(BF16)
                halves.append(_dot(wgt, x2b))
            y_diag = jnp.where(first, halves[0], halves[1])
            st = st_ref[hp_i]
            y_off = _dot(cmb, st.astype(BF16)) * w_in[:, cols]
            y = y_diag + y_off
            if rev:
                y = y + yprev_ref[0, :, cols] + x2 * dsk_ref[:, cols]
            y_ref[0, :, cols] = y
            st_ref[hp_i] = dec[:, cols] * st + _dot_tn(bmb, (x2 * w_out[:, cols]).astype(BF16))


def _ssd_scan(xbc, dt_cols, dt_rows, prm_rows, prm_cols, d_inner, ctx_len, rev, y_prev=None, d_skip=None):
    b, s, _ = xbc.shape
    tc = SSD_CHUNK
    n_heads = d_inner // SSD_HEAD_DIM
    nbc = SSD_GROUPS * SSD_STATE
    n_chunks = s // tc
    ctx_chunks = ctx_len // tc
    d = 1 if rev else 0
    if rev:
        chunk = lambda t: jnp.where(t < ctx_chunks, ctx_chunks - 1 - t, n_chunks - 1 - (t - ctx_chunks))
    else:
        chunk = lambda t: t
    in_specs = [
        pl.BlockSpec((1, tc, d_inner), lambda i, t: (i, chunk(t), 0)),
        pl.BlockSpec((1, tc, nbc), lambda i, t: (i, chunk(t), d_inner // nbc)),
        pl.BlockSpec((1, tc, nbc), lambda i, t: (i, chunk(t), d_inner // nbc + 1)),
        pl.BlockSpec((1, 1, tc, n_heads), lambda i, t: (i, d, chunk(t), 0)),
        pl.BlockSpec((1, 1, n_heads, tc), lambda i, t: (i, d, 0, chunk(t))),
        _const_spec(prm_rows.shape),
        _const_spec(prm_cols.shape),
    ]
    args = [xbc, xbc, xbc, dt_cols, dt_rows, prm_rows, prm_cols]
    if rev:
        in_specs += [pl.BlockSpec((1, tc, d_inner), lambda i, t: (i, chunk(t), 0)), _const_spec((1, d_inner))]
        args += [y_prev, d_skip]
    return pl.pallas_call(
        functools.partial(_ssd_kernel, rev=rev, n_heads=n_heads),
        grid=(b, n_chunks),
        in_specs=in_specs,
        out_specs=pl.BlockSpec((1, tc, d_inner), lambda i, t: (i, chunk(t), 0)),
        out_shape=jax.ShapeDtypeStruct((b, s, d_inner), F32),
        scratch_shapes=[pltpu.VMEM((n_heads // 2, SSD_STATE, 2 * SSD_HEAD_DIM), F32)],
        compiler_params=_params("parallel", "arbitrary"),
        name="ssd_rev" if rev else "ssd_fwd",
    )(*args)


def _ssd_out_kernel(y_ref, gt_ref, x_ref, md_ref, g_ref, ng_ref, w_ref, o_ref):
    gt = gt_ref[0]
    yg = y_ref[0] * (gt * _sigmoid(gt))
    width = yg.shape[1] // SSD_GROUPS
    r = jnp.zeros(x_ref.shape[1:], F32)
    for k in range(SSD_GROUPS):
        seg = _rms(yg[:, k * width:(k + 1) * width], ng_ref[:, k * width:(k + 1) * width])
        r = r + _dot(seg.astype(BF16), w_ref[k * width:(k + 1) * width, :])
    o_ref[0] = x_ref[0] + md_ref[0, 0][2:3] * _rms(r, g_ref[1:2])


def _ssd_out(y, gate, xs, md, g, norm_g, w, ctx_tiles=1):
    b, s, d = xs.shape
    n = y.shape[-1]
    return pl.pallas_call(
        _ssd_out_kernel,
        grid=(b, s // ROW_TILE),
        in_specs=[_row_spec(n), _row_spec(n), _row_spec(d), _mod_spec(d, ctx_tiles), _const_spec(g.shape),
                  _const_spec((1, n)), _const_spec(w.shape)],
        out_specs=_row_spec(d),
        out_shape=jax.ShapeDtypeStruct((b, s, d), F32),
        compiler_params=_params("parallel", "parallel"),
        name="ssd_out",
    )(y, gate, xs, md, g, norm_g.reshape(1, n), w)


def _na_kernel(q_ref, k_ref, v_ref, bias_ref, o_ref, *, ctx_len, grid_rows):
    r = pl.program_id(1)
    hd = NA_HEAD_DIM
    pair = 2 * hd
    win = NA_WIN_ROWS * GRID_W
    r0 = jnp.clip(r - NA_WIN_ROWS // 2, 0, grid_rows - NA_WIN_ROWS)
    base = pl.multiple_of(ctx_len + r0 * GRID_W, GRID_W)
    lane = lax.broadcasted_iota(jnp.int32, (GRID_W, pair), 1)
    first = lane < hd
    scale = hd ** -0.5
    for hp_i in range(q_ref.shape[2] // pair):
        cols = slice(hp_i * pair, (hp_i + 1) * pair)
        q2 = q_ref[0, :, cols] * scale
        kw = k_ref[0, pl.ds(base, win), cols]
        vw = v_ref[0, pl.ds(base, win), cols]
        kc = k_ref[0, 0:ctx_len, cols]
        vc = v_ref[0, 0:ctx_len, cols]
        halves = []
        for s_i in range(2):
            qm = jnp.where(first if s_i == 0 else ~first, q2, 0.0).astype(BF16)
            s_loc = _dot_nt(qm, kw) + bias_ref[0, 2 * hp_i + s_i]
            s_ctx = _dot_nt(qm, kc)
            m = jnp.maximum(jnp.max(s_loc, axis=-1, keepdims=True), jnp.max(s_ctx, axis=-1, keepdims=True))
            p_loc = jnp.exp(s_loc - m)
            p_ctx = jnp.exp(s_ctx - m)
            denom = jnp.sum(p_loc, axis=-1, keepdims=True) + jnp.sum(p_ctx, axis=-1, keepdims=True)
            halves.append((_dot(p_loc.astype(BF16), vw) + _dot(p_ctx.astype(BF16), vc)) / denom)
        o_ref[0, :, cols] = jnp.where(first, halves[0], halves[1]).astype(o_ref.dtype)


def _na_bias_table(rpb):
    dr = np.arange(NA_WIN_ROWS)[:, None, None, None]
    qc = np.arange(GRID_W)[None, :, None, None]
    wi = np.arange(NA_WIN_ROWS)[None, None, :, None]
    kc = np.arange(GRID_W)[None, None, None, :]
    start = np.clip(qc - NA_WIN_COLS // 2, 0, GRID_W - NA_WIN_COLS)
    inside = np.broadcast_to((kc >= start) & (kc < start + NA_WIN_COLS), (NA_WIN_ROWS, GRID_W, NA_WIN_ROWS, GRID_W))
    row_rel = np.broadcast_to(wi - dr + (NA_WIN_ROWS - 1), inside.shape)
    col_rel = np.broadcast_to(np.clip(kc - qc + (NA_WIN_COLS - 1), 0, 2 * NA_WIN_COLS - 2), inside.shape)
    tbl = rpb.astype(F32)[:, row_rel, col_rel]
    tbl = jnp.where(jnp.asarray(inside)[None], tbl, NEG_BIG)
    return jnp.transpose(tbl, (1, 0, 2, 3, 4)).reshape(NA_WIN_ROWS, rpb.shape[0], GRID_W, NA_WIN_ROWS * GRID_W)


def _na_attention(q, k, v, bias, ctx_len):
    b, s, d = q.shape
    n_lat = s - ctx_len
    grid_rows = n_lat // GRID_W
    n_heads = d // NA_HEAD_DIM
    q_off = ctx_len // GRID_W

    def bias_index(i, r):
        return (r - jnp.clip(r - NA_WIN_ROWS // 2, 0, grid_rows - NA_WIN_ROWS), 0, 0, 0)

    return pl.pallas_call(
        functools.partial(_na_kernel, ctx_len=ctx_len, grid_rows=grid_rows),
        grid=(b, grid_rows),
        in_specs=[
            pl.BlockSpec((1, GRID_W, d), lambda i, r: (i, q_off + r, 0)),
            pl.BlockSpec((1, s, d), lambda i, r: (i, 0, 0)),
            pl.BlockSpec((1, s, d), lambda i, r: (i, 0, 0)),
            pl.BlockSpec((1, n_heads, GRID_W, NA_WIN_ROWS * GRID_W), bias_index),
        ],
        out_specs=pl.BlockSpec((1, GRID_W, d), lambda i, r: (i, r, 0)),
        out_shape=jax.ShapeDtypeStruct((b, n_lat, d), BF16),
        compiler_params=_params("parallel", "arbitrary"),
        name="na_attention",
    )(q, k, v, bias)


def _forward(x, c, ctx, c_ctx, ada_w, ada_b, norm_g, mlp_w1, mlp_w2,
             gqa_w_qkv, gqa_q_norm, gqa_k_norm, gqa_w_o,
             s5_a_re, s5_a_im, s5_log_dt, s5_b_re, s5_b_im, s5_c_re, s5_c_im, s5_d, s5_glu_w, s5_glu_b,
             ssd_w_in, ssd_conv_w, ssd_conv_b, ssd_dt_bias, ssd_a_log, ssd_d, ssd_norm_g, ssd_w_out,
             na_w_qkv, na_rpb, na_w_o):
    bsz, n_lat, d = x.shape
    ctx_len = ctx.shape[1]
    depth = ada_w.shape[0]
    n_mod = ada_w.shape[2] // d
    assert depth == 4 and ctx_len == ROW_TILE and n_lat % ROW_TILE == 0 and n_lat % GRID_W == 0
    assert n_lat // GRID_W >= NA_WIN_ROWS

    cond = jnp.concatenate([c, c_ctx[None]], axis=0)
    cond = jnp.pad(cond, ((0, -cond.shape[0] % 8), (0, 0)))
    ada = _ada(cond, ada_w, ada_b)

    xs = jnp.concatenate([ctx, x], axis=1)
    streams = []
    for i in range(depth):
        kind = i % 4
        j = i // 4
        mod = ada[i, :bsz].reshape(bsz, n_mod, d)
        modc = jnp.broadcast_to(ada[i, bsz].reshape(1, n_mod, d), (bsz, n_mod, d))
        md = jnp.pad(jnp.stack([modc, mod], axis=1), ((0, 0), (0, 0), (0, 8 - n_mod), (0, 0)))
        g = jnp.pad(norm_g[i], ((0, 4), (0, 0)))
        w1 = mlp_w1[i].astype(BF16)
        w2 = mlp_w2[i].astype(BF16)
        ctx_tiles = 1
        if kind == 0:
            w_qkv = gqa_w_qkv[j].astype(BF16)
            n = w_qkv.shape[1]
            (qkv,) = _nm_matmul(xs, md, g, w_qkv, ((0, n),), (F32,))
            o = _gqa_attention(qkv, gqa_q_norm[j], gqa_k_norm[j], ctx_len)
            xs = _proj_res(o, xs, md, g, gqa_w_o[j].astype(BF16))
        elif kind == 1:
            h = _nm(xs, md, g, BF16)
            tables = _s5_tables(s5_a_re[j], s5_a_im[j], s5_log_dt[j], s5_b_re[j], s5_b_im[j],
                                s5_c_re[j], s5_c_im[j], s5_d[j])
            y = _s5_core(h, tables, ctx_len)
            xs = _glu_res(y, xs, md, g, s5_glu_w[j].astype(BF16), s5_glu_b[j])
        elif kind == 2:
            w_in = ssd_w_in[j].astype(BF16)
            d_inner = ssd_w_out.shape[1]
            n_heads = d_inner // SSD_HEAD_DIM
            conv_ch = ssd_conv_w.shape[2]
            splits = ((0, d_inner), (d_inner, d_inner + conv_ch), (d_inner + conv_ch, w_in.shape[1]))
            gate, pre, dt_raw = _nm_matmul(xs, md, g, w_in, splits, (F32, F32, F32))
            conv_w = jnp.pad(ssd_conv_w[j], ((0, 8 - ssd_conv_w.shape[1]), (0, 0)))
            xbc = _conv_silu(pre, conv_w, ssd_conv_b[j], ctx_len, ssd_conv_w.shape[1])
            dt_cols = dt_raw.reshape(bsz, -1, 2, n_heads).transpose(0, 2, 1, 3)
            dt_rows = dt_cols.transpose(0, 1, 3, 2)
            d_skip = jnp.repeat(ssd_d[j].astype(F32), SSD_HEAD_DIM).reshape(1, d_inner)
            y = None
            for direction in (0, 1):
                prm = jnp.stack([ssd_dt_bias[j, direction], ssd_a_log[j, direction]]).astype(F32)
                prm_rows = jnp.pad(prm, ((0, 6), (0, 0)))
                prm_cols = jnp.pad(prm.T, ((0, 0), (0, 6)))
                y = _ssd_scan(xbc, dt_cols, dt_rows, prm_rows, prm_cols, d_inner, ctx_len, direction == 1,
                              y_prev=y, d_skip=d_skip)
            xs = _ssd_out(y, gate, xs, md, g, ssd_norm_g[j], ssd_w_out[j].astype(BF16))
        else:
            w_qkv = na_w_qkv[j].astype(BF16)
            q, k, v = _nm_matmul(xs, md, g, w_qkv, ((0, d), (d, 2 * d), (2 * d, 3 * d)), (F32, BF16, BF16))
            o = _na_attention(q, k, v, _na_bias_table(na_rpb[j]), ctx_len)
            xs = xs[:, ctx_len:]
            ctx_tiles = 0
            xs = _proj_res(o, xs, md, g, na_w_o[j].astype(BF16), ctx_tiles=ctx_tiles)
        xs = _mlp(xs, md, g, w1, w2, ctx_tiles=ctx_tiles)
        streams.append(xs)
    return streams


def kernel(x, c, ctx, c_ctx, ada_w, ada_b, norm_g, mlp_w1, mlp_w2, gqa_w_qkv, gqa_q_norm, gqa_k_norm, gqa_w_o, s5_a_re, s5_a_im, s5_log_dt, s5_b_re, s5_b_im, s5_c_re, s5_c_im, s5_d, s5_glu_w, s5_glu_b, ssd_w_in, ssd_conv_w, ssd_conv_b, ssd_dt_bias, ssd_a_log, ssd_d, ssd_norm_g, ssd_w_out, na_w_qkv, na_rpb, na_w_o):
    return _forward(x, c, ctx, c_ctx, ada_w, ada_b, norm_g, mlp_w1, mlp_w2, gqa_w_qkv, gqa_q_norm, gqa_k_norm,
                    gqa_w_o, s5_a_re, s5_a_im, s5_log_dt, s5_b_re, s5_b_im, s5_c_re, s5_c_im, s5_d, s5_glu_w,
                    s5_glu_b, ssd_w_in, ssd_conv_w, ssd_conv_b, ssd_dt_bias, ssd_a_log, ssd_d, ssd_norm_g,
                    ssd_w_out, na_w_qkv, na_rpb, na_w_o)[-1]
```

```python
import functools

import numpy as np
import jax
import jax.numpy as jnp
from jax import lax
from jax.experimental import pallas as pl
from jax.experimental.pallas import tpu as pltpu

F32 = jnp.float32
BF16 = jnp.bfloat16

NORM_EPS = 1e-6
GRID_W = 64
ROW_TILE = 256
VMEM_LIMIT_BYTES = 56 * 1024 * 1024
NEG_BIG = -1e30

GQA_HEAD_DIM = 128
GQA_KV_HEADS = 2
ROPE_THETA = 10000.0
S5_GROUP = 16
S5_STATE = 64
S5_CHUNK = 16
SSD_HEAD_DIM = 64
SSD_GROUPS = 4
SSD_STATE = 128
SSD_CHUNK = 128
NA_HEAD_DIM = 64
NA_WIN_ROWS = 8
NA_WIN_COLS = 16


def _params(*semantics):
    return pltpu.CompilerParams(dimension_semantics=semantics, vmem_limit_bytes=VMEM_LIMIT_BYTES)


def _rms(y, g):
    return y * lax.rsqrt(jnp.mean(y * y, axis=-1, keepdims=True) + NORM_EPS) * g


def _dot(a, b):
    return jnp.dot(a, b, preferred_element_type=F32)


def _dot_nt(a, b):
    return lax.dot_general(a, b, (((1,), (1,)), ((), ())), preferred_element_type=F32)


def _dot_tn(a, b):
    return lax.dot_general(a, b, (((0,), (0,)), ((), ())), preferred_element_type=F32)


def _sigmoid(z):
    return 1.0 / (1.0 + jnp.exp(-z))


def _const_spec(shape):
    nd = len(shape)
    return pl.BlockSpec(shape, lambda *_: (0,) * nd)


def _mod_spec(d, ctx_tiles):
    return pl.BlockSpec((1, 1, 8, d), lambda b, t: (b, jnp.where(t < ctx_tiles, 0, 1), 0, 0))


def _row_spec(n, tm=ROW_TILE):
    return pl.BlockSpec((1, tm, n), lambda b, t: (b, t, 0))


def _ada_kernel(c_ref, w_ref, b_ref, o_ref):
    cc = c_ref[...]
    s = cc * _sigmoid(cc)
    o_ref[0] = _dot(s.astype(BF16), w_ref[0].astype(BF16)) + b_ref[0]


def _ada(cond, ada_w, ada_b):
    depth, d, n = ada_w.shape
    rows = cond.shape[0]
    tn = 1536
    return pl.pallas_call(
        _ada_kernel,
        grid=(depth, n // tn),
        in_specs=[
            pl.BlockSpec((rows, d), lambda i, j: (0, 0)),
            pl.BlockSpec((1, d, tn), lambda i, j: (i, 0, j)),
            pl.BlockSpec((1, 1, tn), lambda i, j: (i, 0, j)),
        ],
        out_specs=pl.BlockSpec((1, rows, tn), lambda i, j: (i, 0, j)),
        out_shape=jax.ShapeDtypeStruct((depth, rows, n), F32),
        compiler_params=_params("parallel", "parallel"),
        name="ada",
    )(cond, ada_w, ada_b.reshape(depth, 1, n))


def _nm_matmul_kernel(x_ref, md_ref, g_ref, w_ref, *o_refs, splits):
    md = md_ref[0, 0]
    h = _rms(x_ref[0], g_ref[0:1]) * (1.0 + md[1:2]) + md[0:1]
    hb = h.astype(BF16)
    for o_ref, (lo, hi) in zip(o_refs, splits):
        o_ref[0] = _dot(hb, w_ref[:, lo:hi]).astype(o_ref.dtype)


def _nm_matmul(xs, md, g, w, splits, dtypes, ctx_tiles=1):
    b, s, d = xs.shape
    out_shape = [jax.ShapeDtypeStruct((b, s, hi - lo), dt) for (lo, hi), dt in zip(splits, dtypes)]
    return pl.pallas_call(
        functools.partial(_nm_matmul_kernel, splits=splits),
        grid=(b, s // ROW_TILE),
        in_specs=[_row_spec(d), _mod_spec(d, ctx_tiles), _const_spec(g.shape), _const_spec(w.shape)],
        out_specs=[_row_spec(hi - lo) for lo, hi in splits],
        out_shape=out_shape,
        compiler_params=_params("parallel", "parallel"),
        name="nm_matmul",
    )(xs, md, g, w)


def _nm_kernel(x_ref, md_ref, g_ref, o_ref):
    md = md_ref[0, 0]
    h = _rms(x_ref[0], g_ref[0:1]) * (1.0 + md[1:2]) + md[0:1]
    o_ref[0] = h.astype(o_ref.dtype)


def _nm(xs, md, g, dtype, ctx_tiles=1):
    b, s, d = xs.shape
    return pl.pallas_call(
        _nm_kernel,
        grid=(b, s // ROW_TILE),
        in_specs=[_row_spec(d), _mod_spec(d, ctx_tiles), _const_spec(g.shape)],
        out_specs=_row_spec(d),
        out_shape=jax.ShapeDtypeStruct((b, s, d), dtype),
        compiler_params=_params("parallel", "parallel"),
        name="nm",
    )(xs, md, g)


def _mlp_kernel(x_ref, md_ref, g_ref, w1_ref, w2_ref, o_ref, *, ff_chunk):
    x = x_ref[0]
    md = md_ref[0, 0]
    hb = (_rms(x, g_ref[2:3]) * (1.0 + md[4:5]) + md[3:4]).astype(BF16)
    acc = jnp.zeros(x.shape, F32)
    for lo in range(0, w1_ref.shape[1], ff_chunk):
        a = _dot(hb, w1_ref[:, lo:lo + ff_chunk])
        a = jnp.square(jnp.maximum(a, 0.0)).astype(BF16)
        acc = acc + _dot(a, w2_ref[lo:lo + ff_chunk, :])
    o_ref[0] = x + md[5:6] * _rms(acc, g_ref[3:4])


def _mlp(xs, md, g, w1, w2, ctx_tiles=1):
    b, s, d = xs.shape
    return pl.pallas_call(
        functools.partial(_mlp_kernel, ff_chunk=1024),
        grid=(b, s // ROW_TILE),
        in_specs=[_row_spec(d), _mod_spec(d, ctx_tiles), _const_spec(g.shape),
                  _const_spec(w1.shape), _const_spec(w2.shape)],
        out_specs=_row_spec(d),
        out_shape=jax.ShapeDtypeStruct((b, s, d), F32),
        compiler_params=_params("parallel", "parallel"),
        name="mlp",
    )(xs, md, g, w1, w2)


def _proj_res_kernel(y_ref, x_ref, md_ref, g_ref, w_ref, o_ref):
    r = _dot(y_ref[0].astype(BF16), w_ref[...])
    o_ref[0] = x_ref[0] + md_ref[0, 0][2:3] * _rms(r, g_ref[1:2])


def _proj_res(y, xs, md, g, w, ctx_tiles=1):
    b, s, d = xs.shape
    return pl.pallas_call(
        _proj_res_kernel,
        grid=(b, s // ROW_TILE),
        in_specs=[_row_spec(y.shape[-1]), _row_spec(d), _mod_spec(d, ctx_tiles), _const_spec(g.shape),
                  _const_spec(w.shape)],
        out_specs=_row_spec(d),
        out_shape=jax.ShapeDtypeStruct((b, s, d), F32),
        compiler_params=_params("parallel", "parallel"),
        name="proj_res",
    )(y, xs, md, g, w)


def _rope(x, cos, sin_signed):
    lane = lax.broadcasted_iota(jnp.int32, x.shape, 1)
    quarter = GQA_HEAD_DIM // 4
    partner = jnp.where((lane % (2 * quarter)) < quarter,
                        pltpu.roll(x, GQA_HEAD_DIM - quarter, 1), pltpu.roll(x, quarter, 1))
    return x * cos + partner * sin_signed


SOFTMAX_ROW_BLOCK = 16


def _softmax_rows(s):
    ps, sums = [], []
    for lo in range(0, s.shape[0], SOFTMAX_ROW_BLOCK):
        sb = s[lo:lo + SOFTMAX_ROW_BLOCK]
        e = jnp.exp(sb - jnp.max(sb, axis=-1, keepdims=True))
        sums.append(jnp.sum(e, axis=-1, keepdims=True))
        ps.append(e.astype(BF16))
    return jnp.concatenate(ps, axis=0), jnp.concatenate(sums, axis=0)


def _gqa_kernel(q_ref, k_ref, v_ref, cq_ref, sq_ref, ck_ref, sk_ref, qn_ref, kn_ref, o_ref, kb_ref, vb_ref,
                *, ctx_len, group):
    t = pl.program_id(2)
    tq = q_ref.shape[1]
    s_all = k_ref.shape[1]
    hd = GQA_HEAD_DIM

    @pl.when(t == 0)
    def _():
        kn = _rms(k_ref[0], kn_ref[...])
        kb_ref[...] = _rope(kn, ck_ref[...], sk_ref[...]).astype(BF16)
        vb_ref[...] = v_ref[0].astype(BF16)

    cos = cq_ref[...]
    sin = sq_ref[...]
    scale = hd ** -0.5

    def attend(n_keys):
        kb = kb_ref[0:n_keys]
        vb = vb_ref[0:n_keys]
        for g in range(group):
            q = q_ref[0, :, g * hd:(g + 1) * hd]
            qb = (_rope(_rms(q, qn_ref[...]), cos, sin) * scale).astype(BF16)
            p, denom = _softmax_rows(_dot_nt(qb, kb))
            o_ref[0, :, g * hd:(g + 1) * hd] = (_dot(p, vb) / denom).astype(o_ref.dtype)

    @pl.when(t * tq < ctx_len)
    def _():
        attend(ctx_len)

    @pl.when(t * tq >= ctx_len)
    def _():
        attend(s_all)


def _rope_tables(n_lat, ctx_len):
    pos = np.arange(n_lat)
    row = (pos // GRID_W).astype(np.float32)
    col = (pos % GRID_W).astype(np.float32)
    half = GQA_HEAD_DIM // 2
    inv_freq = (1.0 / (ROPE_THETA ** (np.arange(0, half, 2, dtype=np.float32) / half))).astype(np.float32)
    ar = row[:, None] * inv_freq
    ac = col[:, None] * inv_freq
    cos = np.concatenate([np.cos(ar), np.cos(ar), np.cos(ac), np.cos(ac)], axis=1)
    sin = np.concatenate([-np.sin(ar), np.sin(ar), -np.sin(ac), np.sin(ac)], axis=1)
    cos = np.concatenate([np.ones((ctx_len, GQA_HEAD_DIM)), cos], axis=0).astype(np.float32)
    sin = np.concatenate([np.zeros((ctx_len, GQA_HEAD_DIM)), sin], axis=0).astype(np.float32)
    return jnp.asarray(cos), jnp.asarray(sin)


def _gqa_attention(qkv, q_norm, k_norm, ctx_len):
    b, s, n = qkv.shape
    hd = GQA_HEAD_DIM
    n_q = n - 2 * GQA_KV_HEADS * hd
    group = n_q // hd // GQA_KV_HEADS
    cos, sin = _rope_tables(s - ctx_len, ctx_len)
    tq = ROW_TILE
    q_blocks = n_q // hd
    return pl.pallas_call(
        functools.partial(_gqa_kernel, ctx_len=ctx_len, group=group),
        grid=(b, GQA_KV_HEADS, s // tq),
        in_specs=[
            pl.BlockSpec((1, tq, group * hd), lambda i, h, t: (i, t, h)),
            pl.BlockSpec((1, s, hd), lambda i, h, t: (i, 0, q_blocks + h)),
            pl.BlockSpec((1, s, hd), lambda i, h, t: (i, 0, q_blocks + GQA_KV_HEADS + h)),
            pl.BlockSpec((tq, hd), lambda i, h, t: (t, 0)),
            pl.BlockSpec((tq, hd), lambda i, h, t: (t, 0)),
            _const_spec((s, hd)),
            _const_spec((s, hd)),
            _const_spec((1, hd)),
            _const_spec((1, hd)),
        ],
        out_specs=pl.BlockSpec((1, tq, group * hd), lambda i, h, t: (i, t, h)),
        out_shape=jax.ShapeDtypeStruct((b, s, n_q), BF16),
        scratch_shapes=[pltpu.VMEM((s, hd), BF16), pltpu.VMEM((s, hd), BF16)],
        compiler_params=_params("parallel", "parallel", "arbitrary"),
        name="gqa_attention",
    )(qkv, qkv, qkv, cos, sin, cos, sin, q_norm.reshape(1, hd), k_norm.reshape(1, hd))


def _s5_kernel(u_ref, m_ref, p_ref, q_ref, a_ref, y_ref, s_ref, h_ref, *, n_chunks, ctx_chunks, nb):
    half = S5_STATE
    u = u_ref[0]
    s_ref[...] = _dot(u, p_ref[0])
    a_re = a_ref[0, 0:1]
    a_im = a_ref[0, 1:2]
    is_fwd = lax.broadcasted_iota(jnp.int32, (nb, 2 * half), 1) < half

    def step(t, carry):
        h_re, h_im = carry
        k_rev = jnp.where(t < ctx_chunks, ctx_chunks - 1 - t, n_chunks - 1 - (t - ctx_chunks))
        rf = pl.multiple_of(t * nb, nb)
        rr = pl.multiple_of(k_rev * nb, nb)
        h_ref[pl.ds(rf, nb), 0:half] = h_re[:, 0:half]
        h_ref[pl.ds(rr, nb), half:2 * half] = h_re[:, half:2 * half]
        h_ref[pl.ds(rf, nb), 2 * half:3 * half] = h_im[:, 0:half]
        h_ref[pl.ds(rr, nb), 3 * half:4 * half] = h_im[:, half:2 * half]
        s_re = jnp.where(is_fwd, s_ref[pl.ds(rf, nb), 0:2 * half], s_ref[pl.ds(rr, nb), 0:2 * half])
        s_im = jnp.where(is_fwd, s_ref[pl.ds(rf, nb), 2 * half:4 * half], s_ref[pl.ds(rr, nb), 2 * half:4 * half])
        return a_re * h_re - a_im * h_im + s_re, a_re * h_im + a_im * h_re + s_im

    zero = jnp.zeros((nb, 2 * half), F32)
    lax.fori_loop(0, n_chunks, step, (zero, zero))
    y_ref[0] = _dot(u, m_ref[0]) + _dot(h_ref[...].astype(BF16), q_ref[0])


def _s5_tables(a_re, a_im, log_dt, b_re, b_im, c_re, c_im, d_skip):
    tc = S5_CHUNK
    n_groups = a_re.shape[1]
    lam = lax.complex(a_re.astype(F32), a_im.astype(F32))
    dt = jnp.exp(log_dt.astype(F32))[..., None]
    a_bar = jnp.exp(lam * dt)
    b_bar = ((a_bar - 1.0) / lam)[..., None] * lax.complex(b_re.astype(F32), b_im.astype(F32))
    cc = lax.complex(c_re.astype(F32), c_im.astype(F32))
    steps = jnp.arange(tc + 1, dtype=F32)
    apow = jnp.exp((lam * dt)[None] * steps[:, None, None, None])
    kern = jnp.einsum("dgcp,ldgp,dgpe->dlgce", cc, apow[:tc], b_bar).real
    jj = np.arange(tc)[:, None]
    ii = np.arange(tc)[None, :]
    kf = kern[0][np.clip(ii - jj, 0, tc - 1)] * jnp.asarray(ii >= jj, F32)[:, :, None, None, None]
    kr = kern[1][np.clip(jj - ii, 0, tc - 1)] * jnp.asarray(jj >= ii, F32)[:, :, None, None, None]
    m = jnp.transpose(kf + kr, (2, 0, 4, 1, 3))
    skip = d_skip.astype(F32).reshape(n_groups, S5_GROUP)
    eye_t = jnp.eye(tc, dtype=F32)
    eye_c = jnp.eye(S5_GROUP, dtype=F32)
    m = m + skip[:, None, :, None, None] * eye_t[None, :, None, :, None] * eye_c[None, None, :, None, :]
    m = m.reshape(n_groups, tc * S5_GROUP, tc * S5_GROUP)
    pf = jnp.einsum("jgp,gpe->gjep", apow[:tc][::-1, 0], b_bar[0])
    pr = jnp.einsum("jgp,gpe->gjep", apow[:tc, 1], b_bar[1])
    p = jnp.concatenate([pf.real, pr.real, pf.imag, pr.imag], axis=-1).reshape(n_groups, tc * S5_GROUP, 4 * S5_STATE)
    wf = jnp.einsum("gcp,igp->gpic", cc[0], apow[1:, 0])
    wr = jnp.einsum("gcp,igp->gpic", cc[1], apow[1:][::-1, 1])
    q = jnp.concatenate([wf.real, wr.real, -wf.imag, -wr.imag], axis=1).reshape(n_groups, 4 * S5_STATE, tc * S5_GROUP)
    a_tc = apow[tc]
    dec = jnp.stack([jnp.concatenate([a_tc[0].real, a_tc[1].real], axis=-1),
                     jnp.concatenate([a_tc[0].imag, a_tc[1].imag], axis=-1)], axis=1)
    dec = jnp.pad(dec, ((0, 0), (0, 6), (0, 0)))
    return m.astype(BF16), p.astype(BF16), q.astype(BF16), dec


def _s5_core(h, tables, ctx_len):
    b, s, d = h.shape
    tc = S5_CHUNK
    n_groups = d // S5_GROUP
    n_chunks = s // tc
    w = tc * S5_GROUP
    m, p, q, dec = tables
    u = h.reshape(b, n_chunks, tc, n_groups, S5_GROUP).transpose(3, 1, 0, 2, 4).reshape(n_groups, n_chunks * b, w)
    grp = lambda g: (g, 0, 0)
    y = pl.pallas_call(
        functools.partial(_s5_kernel, n_chunks=n_chunks, ctx_chunks=ctx_len // tc, nb=b),
        grid=(n_groups,),
        in_specs=[
            pl.BlockSpec((1, n_chunks * b, w), grp),
            pl.BlockSpec((1, w, w), grp),
            pl.BlockSpec((1, w, 4 * S5_STATE), grp),
            pl.BlockSpec((1, 4 * S5_STATE, w), grp),
            pl.BlockSpec((1, 8, 2 * S5_STATE), grp),
        ],
        out_specs=pl.BlockSpec((1, n_chunks * b, w), grp),
        out_shape=jax.ShapeDtypeStruct((n_groups, n_chunks * b, w), F32),
        scratch_shapes=[pltpu.VMEM((n_chunks * b, 4 * S5_STATE), F32), pltpu.VMEM((n_chunks * b, 4 * S5_STATE), F32)],
        compiler_params=_params("parallel"),
        name="s5_core",
    )(u, m, p, q, dec)
    return y.reshape(n_groups, n_chunks, b, tc, S5_GROUP).transpose(2, 1, 3, 0, 4).reshape(b, s, d)


def _glu_res_kernel(y_ref, x_ref, md_ref, g_ref, w_ref, b_ref, o_ref):
    y = y_ref[0]
    gel = y * (0.5 * (1.0 + jnp.tanh(np.sqrt(2.0 / np.pi).astype(np.float32) * (y + 0.044715 * (y * y * y)))))
    out = gel * _sigmoid(_dot(gel.astype(BF16), w_ref[...]) + b_ref[...])
    o_ref[0] = x_ref[0] + md_ref[0, 0][2:3] * _rms(out, g_ref[1:2])


def _glu_res(y, xs, md, g, w, bias, ctx_tiles=1):
    b, s, d = xs.shape
    return pl.pallas_call(
        _glu_res_kernel,
        grid=(b, s // ROW_TILE),
        in_specs=[_row_spec(d), _row_spec(d), _mod_spec(d, ctx_tiles), _const_spec(g.shape),
                  _const_spec(w.shape), _const_spec((1, d))],
        out_specs=_row_spec(d),
        out_shape=jax.ShapeDtypeStruct((b, s, d), F32),
        compiler_params=_params("parallel", "parallel"),
        name="glu_res",
    )(y, xs, md, g, w, bias.reshape(1, d))


def _conv_silu_kernel(x_ref, prev_ref, next_ref, w_ref, b_ref, o_ref, *, ctx_len, seq_len, width):
    t = pl.program_id(1)
    tm = x_ref.shape[1]
    x = x_ref[0]
    lo = t * tm
    hi = lo + tm
    prev_ok = jnp.where((lo == 0) | (lo == ctx_len), 0.0, 1.0)
    next_ok = jnp.where((hi == ctx_len) | (hi == seq_len), 0.0, 1.0)
    prev = prev_ref[0] * prev_ok
    nxt = next_ref[0] * next_ok
    row = lax.broadcasted_iota(jnp.int32, x.shape, 0)
    half = width // 2
    acc = x * w_ref[half:half + 1] + b_ref[...]
    for k in range(1, half + 1):
        back = pltpu.roll(x, k, 0)
        for r in range(k):
            back = jnp.where(row == r, prev[8 - k + r:8 - k + r + 1], back)
        acc = acc + back * w_ref[half - k:half - k + 1]
        fwd = pltpu.roll(x, tm - k, 0)
        for r in range(k):
            fwd = jnp.where(row == tm - k + r, nxt[r:r + 1], fwd)
        acc = acc + fwd * w_ref[half + k:half + k + 1]
    o_ref[0] = acc * _sigmoid(acc)


def _conv_silu(x, w, bias, ctx_len, width):
    b, s, n = x.shape
    tm = ROW_TILE
    sub = tm // 8
    last = s // 8 - 1
    return pl.pallas_call(
        functools.partial(_conv_silu_kernel, ctx_len=ctx_len, seq_len=s, width=width),
        grid=(b, s // tm),
        in_specs=[
            _row_spec(n),
            pl.BlockSpec((1, 8, n), lambda i, t: (i, jnp.maximum(t * sub - 1, 0), 0)),
            pl.BlockSpec((1, 8, n), lambda i, t: (i, jnp.minimum((t + 1) * sub, last), 0)),
            _const_spec(w.shape),
            _const_spec((1, n)),
        ],
        out_specs=_row_spec(n),
        out_shape=jax.ShapeDtypeStruct((b, s, n), F32),
        compiler_params=_params("parallel", "parallel"),
        name="conv_silu",
    )(x, x, x, w, bias.reshape(1, n))


def _softplus(z):
    return jnp.maximum(z, 0.0) + jnp.log1p(jnp.exp(-jnp.abs(z)))


def _expand_heads(v, e3):
    hi = v.astype(BF16)
    r1 = v - hi.astype(F32)
    mid = r1.astype(BF16)
    lo = (r1 - mid.astype(F32)).astype(BF16)
    return _dot(jnp.concatenate([hi, mid, lo], axis=1), e3)


def _ssd_kernel(*refs, rev, n_heads):
    if rev:
        (xs_ref, bm_ref, cm_ref, dtc_ref, dtr_ref, pr_ref, pc_ref, yprev_ref, dsk_ref, y_ref, st_ref) = refs
    else:
        (xs_ref, bm_ref, cm_ref, dtc_ref, dtr_ref, pr_ref, pc_ref, y_ref, st_ref) = refs
    t = pl.program_id(1)
    tc = xs_ref.shape[1]
    hd = SSD_HEAD_DIM
    pair = 2 * hd
    heads_per_group = n_heads // SSD_GROUPS

    @pl.when(t == 0)
    def _():
        st_ref[...] = jnp.zeros(st_ref.shape, F32)

    dtc = _softplus(dtc_ref[0, 0] + pr_ref[0:1])
    dtr = _softplus(dtr_ref[0, 0] + pc_ref[:, 0:1])
    dac = dtc * -jnp.exp(pr_ref[1:2])
    dar = dtr * -jnp.exp(pc_ref[:, 1:2])
    ii = lax.broadcasted_iota(jnp.int32, (tc, tc), 0)
    jj = lax.broadcasted_iota(jnp.int32, (tc, tc), 1)
    visible = (jj >= ii) if rev else (ii >= jj)
    tri = jnp.where(visible, 1.0, 0.0)
    hp = lax.Precision.HIGHEST
    cum_c = jnp.dot(tri, dac, precision=hp, preferred_element_type=F32)
    cum_r = lax.dot_general(dar, tri, (((1,), (1,)), ((), ())), precision=hp, preferred_element_type=F32)
    tot = cum_c[0:1] if rev else cum_c[tc - 1:tc]

    e_row = lax.broadcasted_iota(jnp.int32, (3 * n_heads, n_heads * hd), 0)
    e_col = lax.broadcasted_iota(jnp.int32, (3 * n_heads, n_heads * hd), 1)
    e3 = jnp.where((e_row % n_heads) == (e_col // hd), 1.0, 0.0).astype(BF16)
    w_out = _expand_heads(dtc * jnp.exp(tot - cum_c), e3)
    w_in = _expand_heads(jnp.exp(cum_c), e3)
    dec = _expand_heads(jnp.broadcast_to(jnp.exp(tot), (8, n_heads)), e3)[0:1]

    xs = xs_ref[0]
    lane = lax.broadcasted_iota(jnp.int32, (tc, pair), 1)
    first = lane < hd
    for g in range(SSD_GROUPS):
        bmb = bm_ref[0, :, g * SSD_STATE:(g + 1) * SSD_STATE].astype(BF16)
        cmb = cm_ref[0, :, g * SSD_STATE:(g + 1) * SSD_STATE].astype(BF16)
        scores = _dot_nt(cmb, bmb)
        for k in range(heads_per_group // 2):
            hp_i = g * (heads_per_group // 2) + k
            cols = slice(hp_i * pair, (hp_i + 1) * pair)
            x2 = xs[:, cols]
            x2b = x2.astype(BF16)
            halves = []
            for s_i in range(2):
                h = 2 * hp_i + s_i
                decay = jnp.exp(jnp.where(visible, cum_c[:, h:h + 1] - cum_r[h:h + 1, :], -jnp.inf))
                wgt = (scores * decay * dtr[h:h + 1, :]).astype(BF16)
                halves.append(_dot(wgt, x2b))
            y_diag = jnp.where(first, halves[0], halves[1])
            st = st_ref[hp_i]
            y_off = _dot(cmb, st.astype(BF16)) * w_in[:, cols]
            y = y_diag + y_off
            if rev:
                y = y + yprev_ref[0, :, cols] + x2 * dsk_ref[:, cols]
            y_ref[0, :, cols] = y
            st_ref[hp_i] = dec[:, cols] * st + _dot_tn(bmb, (x2 * w_out[:, cols]).astype(BF16))


def _ssd_scan(xbc, dt_cols, dt_rows, prm_rows, prm_cols, d_inner, ctx_len, rev, y_prev=None, d_skip=None):
    b, s, _ = xbc.shape
    tc = SSD_CHUNK
    n_heads = d_inner // SSD_HEAD_DIM
    nbc = SSD_GROUPS * SSD_STATE
    n_chunks = s // tc
    ctx_chunks = ctx_len // tc
    d = 1 if rev else 0
    if rev:
        chunk = lambda t: jnp.where(t < ctx_chunks, ctx_chunks - 1 - t, n_chunks - 1 - (t - ctx_chunks))
    else:
        chunk = lambda t: t
    in_specs = [
        pl.BlockSpec((1, tc, d_inner), lambda i, t: (i, chunk(t), 0)),
        pl.BlockSpec((1, tc, nbc), lambda i, t: (i, chunk(t), d_inner // nbc)),
        pl.BlockSpec((1, tc, nbc), lambda i, t: (i, chunk(t), d_inner // nbc + 1)),
        pl.BlockSpec((1, 1, tc, n_heads), lambda i, t: (i, d, chunk(t), 0)),
        pl.BlockSpec((1, 1, n_heads, tc), lambda i, t: (i, d, 0, chunk(t))),
        _const_spec(prm_rows.shape),
        _const_spec(prm_cols.shape),
    ]
    args = [xbc, xbc, xbc, dt_cols, dt_rows, prm_rows, prm_cols]
    if rev:
        in_specs += [pl.BlockSpec((1, tc, d_inner), lambda i, t: (i, chunk(t), 0)), _const_spec((1, d_inner))]
        args += [y_prev, d_skip]
    return pl.pallas_call(
        functools.partial(_ssd_kernel, rev=rev, n_heads=n_heads),
        grid=(b, n_chunks),
        in_specs=in_specs,
        out_specs=pl.BlockSpec((1, tc, d_inner), lambda i, t: (i, chunk(t), 0)),
        out_shape=jax.ShapeDtypeStruct((b, s, d_inner), F32),
        scratch_shapes=[pltpu.VMEM((n_heads // 2, SSD_STATE, 2 * SSD_HEAD_DIM), F32)],
        compiler_params=_params("parallel", "arbitrary"),
        name="ssd_rev" if rev else "ssd_fwd",
    )(*args)


def _ssd_out_kernel(y_ref, gt_ref, x_ref, md_ref, g_ref, ng_ref, w_ref, o_ref):
    gt = gt_ref[0]
    yg = y_ref[0] * (gt * _sigmoid(gt))
    width = yg.shape[1] // SSD_GROUPS
    r = jnp.zeros(x_ref.shape[1:], F32)
    for k in range(SSD_GROUPS):
        seg = _rms(yg[:, k * width:(k + 1) * width], ng_ref[:, k * width:(k + 1) * width])
        r = r + _dot(seg.astype(BF16), w_ref[k * width:(k + 1) * width, :])
    o_ref[0] = x_ref[0] + md_ref[0, 0][2:3] * _rms(r, g_ref[1:2])


def _ssd_out(y, gate, xs, md, g, norm_g, w, ctx_tiles=1):
    b, s, d = xs.shape
    n = y.shape[-1]
    return pl.pallas_call(
        _ssd_out_kernel,
        grid=(b, s // ROW_TILE),
        in_specs=[_row_spec(n), _row_spec(n), _row_spec(d), _mod_spec(d, ctx_tiles), _const_spec(g.shape),
                  _const_spec((1, n)), _const_spec(w.shape)],
        out_specs=_row_spec(d),
        out_shape=jax.ShapeDtypeStruct((b, s, d), F32),
        compiler_params=_params("parallel", "parallel"),
        name="ssd_out",
    )(y, gate, xs, md, g, norm_g.reshape(1, n), w)


def _na_kernel(q_ref, k_ref, v_ref, bias_ref, o_ref, *, ctx_len, grid_rows):
    r = pl.program_id(1)
    hd = NA_HEAD_DIM
    pair = 2 * hd
    n_heads = q_ref.shape[2] // hd
    win = NA_WIN_ROWS * GRID_W
    r0 = jnp.clip(r - NA_WIN_ROWS // 2, 0, grid_rows - NA_WIN_ROWS)
    base = pl.multiple_of(ctx_len + r0 * GRID_W, GRID_W)
    lane = lax.broadcasted_iota(jnp.int32, (GRID_W, pair), 1)
    first = lane < hd
    scale = hd ** -0.5
    s_loc, s_ctx = [], []
    for h in range(n_heads):
        cols = slice((h // 2) * pair, (h // 2 + 1) * pair)
        qm = jnp.where(first if h % 2 == 0 else ~first, q_ref[0, :, cols] * scale, 0.0).astype(BF16)
        s_loc.append(_dot_nt(qm, k_ref[0, pl.ds(base, win), cols]) + bias_ref[0, h])
        s_ctx.append(_dot_nt(qm, k_ref[0, 0:ctx_len, cols]))
    p_loc, p_ctx, inv = [], [], []
    for h in range(n_heads):
        m = jnp.maximum(jnp.max(s_loc[h], axis=-1, keepdims=True), jnp.max(s_ctx[h], axis=-1, keepdims=True))
        e_loc = jnp.exp(s_loc[h] - m)
        e_ctx = jnp.exp(s_ctx[h] - m)
        inv.append(1.0 / (jnp.sum(e_loc, axis=-1, keepdims=True) + jnp.sum(e_ctx, axis=-1, keepdims=True)))
        p_loc.append(e_loc.astype(BF16))
        p_ctx.append(e_ctx.astype(BF16))
    for hp_i in range(n_heads // 2):
        cols = slice(hp_i * pair, (hp_i + 1) * pair)
        vw = v_ref[0, pl.ds(base, win), cols]
        vc = v_ref[0, 0:ctx_len, cols]
        halves = [(_dot(p_loc[h], vw) + _dot(p_ctx[h], vc)) * inv[h] for h in (2 * hp_i, 2 * hp_i + 1)]
        o_ref[0, :, cols] = jnp.where(first, halves[0], halves[1]).astype(o_ref.dtype)


def _na_bias_table(rpb):
    n_heads, n_rr, n_cr = rpb.shape
    qc = np.arange(GRID_W)[:, None]
    kc = np.arange(GRID_W)[None, :]
    start = np.clip(qc - NA_WIN_COLS // 2, 0, GRID_W - NA_WIN_COLS)
    inside = (kc >= start) & (kc < start + NA_WIN_COLS)
    col_rel = kc - qc + (NA_WIN_COLS - 1)
    onehot = ((col_rel[None] == np.arange(n_cr)[:, None, None]) & inside[None]).astype(np.float32)
    t2 = jnp.dot(rpb.astype(F32).reshape(n_heads * n_rr, n_cr), jnp.asarray(onehot.reshape(n_cr, GRID_W * GRID_W)),
                 precision=lax.Precision.HIGHEST).reshape(n_heads, n_rr, GRID_W, GRID_W)
    t2 = jnp.where(jnp.asarray(inside), t2, NEG_BIG)
    tbl = jnp.stack([t2[:, NA_WIN_ROWS - 1 - dr:2 * NA_WIN_ROWS - 1 - dr] for dr in range(NA_WIN_ROWS)])
    return jnp.transpose(tbl, (0, 1, 3, 2, 4)).reshape(NA_WIN_ROWS, n_heads, GRID_W, NA_WIN_ROWS * GRID_W)


def _na_attention(q, k, v, bias, ctx_len):
    b, s, d = q.shape
    n_lat = s - ctx_len
    grid_rows = n_lat // GRID_W
    n_heads = d // NA_HEAD_DIM
    q_off = ctx_len // GRID_W

    def bias_index(i, r):
        return (r - jnp.clip(r - NA_WIN_ROWS // 2, 0, grid_rows - NA_WIN_ROWS), 0, 0, 0)

    return pl.pallas_call(
        functools.partial(_na_kernel, ctx_len=ctx_len, grid_rows=grid_rows),
        grid=(b, grid_rows),
        in_specs=[
            pl.BlockSpec((1, GRID_W, d), lambda i, r: (i, q_off + r, 0)),
            pl.BlockSpec((1, s, d), lambda i, r: (i, 0, 0)),
            pl.BlockSpec((1, s, d), lambda i, r: (i, 0, 0)),
            pl.BlockSpec((1, n_heads, GRID_W, NA_WIN_ROWS * GRID_W), bias_index),
        ],
        out_specs=pl.BlockSpec((1, GRID_W, d), lambda i, r: (i, r, 0)),
        out_shape=jax.ShapeDtypeStruct((b, n_lat, d), BF16),
        compiler_params=_params("parallel", "arbitrary"),
        name="na_attention",
    )(q, k, v, bias)


def _forward(x, c, ctx, c_ctx, ada_w, ada_b, norm_g, mlp_w1, mlp_w2,
             gqa_w_qkv, gqa_q_norm, gqa_k_norm, gqa_w_o,
             s5_a_re, s5_a_im, s5_log_dt, s5_b_re, s5_b_im, s5_c_re, s5_c_im, s5_d, s5_glu_w, s5_glu_b,
             ssd_w_in, ssd_conv_w, ssd_conv_b, ssd_dt_bias, ssd_a_log, ssd_d, ssd_norm_g, ssd_w_out,
             na_w_qkv, na_rpb, na_w_o):
    bsz, n_lat, d = x.shape
    ctx_len = ctx.shape[1]
    depth = ada_w.shape[0]
    n_mod = ada_w.shape[2] // d
    assert depth == 4 and ctx_len == ROW_TILE and n_lat % ROW_TILE == 0 and n_lat % GRID_W == 0
    assert n_lat // GRID_W >= NA_WIN_ROWS

    cond = jnp.concatenate([c, c_ctx[None]], axis=0)
    cond = jnp.pad(cond, ((0, -cond.shape[0] % 8), (0, 0)))
    ada = _ada(cond, ada_w, ada_b)

    xs = jnp.concatenate([ctx, x], axis=1)
    streams = []
    for i in range(depth):
        kind = i % 4
        j = i // 4
        mod = ada[i, :bsz].reshape(bsz, n_mod, d)
        modc = jnp.broadcast_to(ada[i, bsz].reshape(1, n_mod, d), (bsz, n_mod, d))
        md = jnp.pad(jnp.stack([modc, mod], axis=1), ((0, 0), (0, 0), (0, 8 - n_mod), (0, 0)))
        g = jnp.pad(norm_g[i], ((0, 4), (0, 0)))
        w1 = mlp_w1[i].astype(BF16)
        w2 = mlp_w2[i].astype(BF16)
        ctx_tiles = 1
        if kind == 0:
            w_qkv = gqa_w_qkv[j].astype(BF16)
            n = w_qkv.shape[1]
            (qkv,) = _nm_matmul(xs, md, g, w_qkv, ((0, n),), (F32,))
            o = _gqa_attention(qkv, gqa_q_norm[j], gqa_k_norm[j], ctx_len)
            xs = _proj_res(o, xs, md, g, gqa_w_o[j].astype(BF16))
        elif kind == 1:
            h = _nm(xs, md, g, BF16)
            tables = _s5_tables(s5_a_re[j], s5_a_im[j], s5_log_dt[j], s5_b_re[j], s5_b_im[j],
                                s5_c_re[j], s5_c_im[j], s5_d[j])
            y = _s5_core(h, tables, ctx_len)
            xs = _glu_res(y, xs, md, g, s5_glu_w[j].astype(BF16), s5_glu_b[j])
        elif kind == 2:
            w_in = ssd_w_in[j].astype(BF16)
            d_inner = ssd_w_out.shape[1]
            n_heads = d_inner // SSD_HEAD_DIM
            conv_ch = ssd_conv_w.shape[2]
            splits = ((0, d_inner), (d_inner, d_inner + conv_ch), (d_inner + conv_ch, w_in.shape[1]))
            gate, pre, dt_raw = _nm_matmul(xs, md, g, w_in, splits, (F32, F32, F32))
            conv_w = jnp.pad(ssd_conv_w[j], ((0, 8 - ssd_conv_w.shape[1]), (0, 0)))
            xbc = _conv_silu(pre, conv_w, ssd_conv_b[j], ctx_len, ssd_conv_w.shape[1])
            dt_cols = dt_raw.reshape(bsz, -1, 2, n_heads).transpose(0, 2, 1, 3)
            dt_rows = dt_cols.transpose(0, 1, 3, 2)
            d_skip = jnp.repeat(ssd_d[j].astype(F32), SSD_HEAD_DIM).reshape(1, d_inner)
            y = None
            for direction in (0, 1):
                prm = jnp.stack([ssd_dt_bias[j, direction], ssd_a_log[j, direction]]).astype(F32)
                prm_rows = jnp.pad(prm, ((0, 6), (0, 0)))
                prm_cols = jnp.pad(prm.T, ((0, 0), (0, 6)))
                y = _ssd_scan(xbc, dt_cols, dt_rows, prm_rows, prm_cols, d_inner, ctx_len, direction == 1,
                              y_prev=y, d_skip=d_skip)
            xs = _ssd_out(y, gate, xs, md, g, ssd_norm_g[j], ssd_w_out[j].astype(BF16))
        else:
            w_qkv = na_w_qkv[j].astype(BF16)
            q, k, v = _nm_matmul(xs, md, g, w_qkv, ((0, d), (d, 2 * d), (2 * d, 3 * d)), (F32, BF16, BF16))
            o = _na_attention(q, k, v, _na_bias_table(na_rpb[j]), ctx_len)
            xs = xs[:, ctx_len:]
            ctx_tiles = 0
            xs = _proj_res(o, xs, md, g, na_w_o[j].astype(BF16), ctx_tiles=ctx_tiles)
        xs = _mlp(xs, md, g, w1, w2, ctx_tiles=ctx_tiles)
        streams.append(xs)
    return streams


def kernel(x, c, ctx, c_ctx, ada_w, ada_b, norm_g, mlp_w1, mlp_w2, gqa_w_qkv, gqa_q_norm, gqa_k_norm, gqa_w_o, s5_a_re, s5_a_im, s5_log_dt, s5_b_re, s5_b_im, s5_c_re, s5_c_im, s5_d, s5_glu_w, s5_glu_b, ssd_w_in, ssd_conv_w, ssd_conv_b, ssd_dt_bias, ssd_a_log, ssd_d, ssd_norm_g, ssd_w_out, na_w_qkv, na_rpb, na_w_o):
    return _forward(x, c, ctx, c_ctx, ada_w, ada_b, norm_g, mlp_w1, mlp_w2, gqa_w_qkv, gqa_q_norm, gqa_k_norm,
                    gqa_w_o, s5_a_re, s5_a_im, s5_log_dt, s5_b_re, s5_b_im, s5_c_re, s5_c_im, s5_d, s5_glu_w,
                    s5_glu_b, ssd_w_in, ssd_conv_w, ssd_conv_b, ssd_dt_bias, ssd_a_log, ssd_d, ssd_norm_g,
                    ssd_w_out, na_w_qkv, na_rpb, na_w_o)[-1]
```

```python
import functools

import numpy as np
import jax
import jax.numpy as jnp
from jax import lax
from jax.experimental import pallas as pl
from jax.experimental.pallas import tpu as pltpu

F32 = jnp.float32
BF16 = jnp.bfloat16

NORM_EPS = 1e-6
GRID_W = 64
ROW_TILE = 256
VMEM_LIMIT_BYTES = 56 * 1024 * 1024
NEG_BIG = -1e30

GQA_HEAD_DIM = 128
GQA_KV_HEADS = 2
ROPE_THETA = 10000.0
S5_GROUP = 16
S5_STATE = 64
S5_CHUNK = 16
SSD_HEAD_DIM = 64
SSD_GROUPS = 4
SSD_STATE = 128
SSD_CHUNK = 128
NA_HEAD_DIM = 64
NA_WIN_ROWS = 8
NA_WIN_COLS = 16


def _params(*semantics):
    return pltpu.CompilerParams(dimension_semantics=semantics, vmem_limit_bytes=VMEM_LIMIT_BYTES)


def _rms(y, g):
    return y * lax.rsqrt(jnp.mean(y * y, axis=-1, keepdims=True) + NORM_EPS) * g


def _dot(a, b):
    return jnp.dot(a, b, preferred_element_type=F32)


def _dot_nt(a, b):
    return lax.dot_general(a, b, (((1,), (1,)), ((), ())), preferred_element_type=F32)


def _dot_tn(a, b):
    return lax.dot_general(a, b, (((0,), (0,)), ((), ())), preferred_element_type=F32)


def _sigmoid(z):
    return 1.0 / (1.0 + jnp.exp(-z))


def _const_spec(shape):
    nd = len(shape)
    return pl.BlockSpec(shape, lambda *_: (0,) * nd)


def _mod_spec(d, ctx_tiles):
    return pl.BlockSpec((1, 1, 8, d), lambda b, t: (b, jnp.where(t < ctx_tiles, 0, 1), 0, 0))


def _row_spec(n, tm=ROW_TILE):
    return pl.BlockSpec((1, tm, n), lambda b, t: (b, t, 0))


def _ada_kernel(c_ref, w_ref, b_ref, o_ref):
    cc = c_ref[...]
    s = cc * _sigmoid(cc)
    o_ref[0] = _dot(s.astype(BF16), w_ref[0].astype(BF16)) + b_ref[0]


def _ada(cond, ada_w, ada_b):
    depth, d, n = ada_w.shape
    rows = cond.shape[0]
    tn = 1536
    return pl.pallas_call(
        _ada_kernel,
        grid=(depth, n // tn),
        in_specs=[
            pl.BlockSpec((rows, d), lambda i, j: (0, 0)),
            pl.BlockSpec((1, d, tn), lambda i, j: (i, 0, j)),
            pl.BlockSpec((1, 1, tn), lambda i, j: (i, 0, j)),
        ],
        out_specs=pl.BlockSpec((1, rows, tn), lambda i, j: (i, 0, j)),
        out_shape=jax.ShapeDtypeStruct((depth, rows, n), F32),
        compiler_params=_params("parallel", "parallel"),
        name="ada",
    )(cond, ada_w, ada_b.reshape(depth, 1, n))


def _nm_matmul_kernel(x_ref, md_ref, g_ref, w_ref, *o_refs, splits):
    md = md_ref[0, 0]
    h = _rms(x_ref[0], g_ref[0:1]) * (1.0 + md[1:2]) + md[0:1]
    hb = h.astype(BF16)
    for o_ref, (lo, hi) in zip(o_refs, splits):
        o_ref[0] = _dot(hb, w_ref[:, lo:hi]).astype(o_ref.dtype)


def _nm_matmul(xs, md, g, w, splits, dtypes, ctx_tiles=1):
    b, s, d = xs.shape
    out_shape = [jax.ShapeDtypeStruct((b, s, hi - lo), dt) for (lo, hi), dt in zip(splits, dtypes)]
    return pl.pallas_call(
        functools.partial(_nm_matmul_kernel, splits=splits),
        grid=(b, s // ROW_TILE),
        in_specs=[_row_spec(d), _mod_spec(d, ctx_tiles), _const_spec(g.shape), _const_spec(w.shape)],
        out_specs=[_row_spec(hi - lo) for lo, hi in splits],
        out_shape=out_shape,
        compiler_params=_params("parallel", "parallel"),
        name="nm_matmul",
    )(xs, md, g, w)


def _nm_kernel(x_ref, md_ref, g_ref, o_ref):
    md = md_ref[0, 0]
    h = _rms(x_ref[0], g_ref[0:1]) * (1.0 + md[1:2]) + md[0:1]
    o_ref[0] = h.astype(o_ref.dtype)


def _nm(xs, md, g, dtype, ctx_tiles=1):
    b, s, d = xs.shape
    return pl.pallas_call(
        _nm_kernel,
        grid=(b, s // ROW_TILE),
        in_specs=[_row_spec(d), _mod_spec(d, ctx_tiles), _const_spec(g.shape)],
        out_specs=_row_spec(d),
        out_shape=jax.ShapeDtypeStruct((b, s, d), dtype),
        compiler_params=_params("parallel", "parallel"),
        name="nm",
    )(xs, md, g)


def _mlp_kernel(x_ref, md_ref, g_ref, w1_ref, w2_ref, o_ref, *, ff_chunk):
    x = x_ref[0]
    md = md_ref[0, 0]
    hb = (_rms(x, g_ref[2:3]) * (1.0 + md[4:5]) + md[3:4]).astype(BF16)
    acc = jnp.zeros(x.shape, F32)
    for lo in range(0, w1_ref.shape[1], ff_chunk):
        a = _dot(hb, w1_ref[:, lo:lo + ff_chunk])
        a = jnp.square(jnp.maximum(a, 0.0)).astype(BF16)
        acc = acc + _dot(a, w2_ref[lo:lo + ff_chunk, :])
    o_ref[0] = x + md[5:6] * _rms(acc, g_ref[3:4])


def _mlp(xs, md, g, w1, w2, ctx_tiles=1):
    b, s, d = xs.shape
    return pl.pallas_call(
        functools.partial(_mlp_kernel, ff_chunk=1024),
        grid=(b, s // ROW_TILE),
        in_specs=[_row_spec(d), _mod_spec(d, ctx_tiles), _const_spec(g.shape),
                  _const_spec(w1.shape), _const_spec(w2.shape)],
        out_specs=_row_spec(d),
        out_shape=jax.ShapeDtypeStruct((b, s, d), F32),
        compiler_params=_params("parallel", "parallel"),
        name="mlp",
    )(xs, md, g, w1, w2)


def _proj_res_kernel(y_ref, x_ref, md_ref, g_ref, w_ref, o_ref):
    r = _dot(y_ref[0].astype(BF16), w_ref[...])
    o_ref[0] = x_ref[0] + md_ref[0, 0][2:3] * _rms(r, g_ref[1:2])


def _proj_res(y, xs, md, g, w, ctx_tiles=1):
    b, s, d = xs.shape
    return pl.pallas_call(
        _proj_res_kernel,
        grid=(b, s // ROW_TILE),
        in_specs=[_row_spec(y.shape[-1]), _row_spec(d), _mod_spec(d, ctx_tiles), _const_spec(g.shape),
                  _const_spec(w.shape)],
        out_specs=_row_spec(d),
        out_shape=jax.ShapeDtypeStruct((b, s, d), F32),
        compiler_params=_params("parallel", "parallel"),
        name="proj_res",
    )(y, xs, md, g, w)


def _rope(x, cos, sin_signed):
    lane = lax.broadcasted_iota(jnp.int32, x.shape, 1)
    quarter = GQA_HEAD_DIM // 4
    partner = jnp.where((lane % (2 * quarter)) < quarter,
                        pltpu.roll(x, GQA_HEAD_DIM - quarter, 1), pltpu.roll(x, quarter, 1))
    return x * cos + partner * sin_signed


SOFTMAX_ROW_BLOCK = 16


def _softmax_rows(s):
    ps, sums = [], []
    for lo in range(0, s.shape[0], SOFTMAX_ROW_BLOCK):
        sb = s[lo:lo + SOFTMAX_ROW_BLOCK]
        e = jnp.exp(sb - jnp.max(sb, axis=-1, keepdims=True))
        sums.append(jnp.sum(e, axis=-1, keepdims=True))
        ps.append(e.astype(BF16))
    return jnp.concatenate(ps, axis=0), jnp.concatenate(sums, axis=0)


def _gqa_kernel(q_ref, k_ref, v_ref, cq_ref, sq_ref, ck_ref, sk_ref, qn_ref, kn_ref, o_ref, kb_ref, vb_ref,
                *, ctx_len, group):
    t = pl.program_id(2)
    tq = q_ref.shape[1]
    s_all = k_ref.shape[1]
    hd = GQA_HEAD_DIM

    @pl.when(t == 0)
    def _():
        kn = _rms(k_ref[0], kn_ref[...])
        kb_ref[...] = _rope(kn, ck_ref[...], sk_ref[...]).astype(BF16)
        vb_ref[...] = v_ref[0].astype(BF16)

    cos = cq_ref[...]
    sin = sq_ref[...]
    scale = hd ** -0.5

    def attend(n_keys):
        kb = kb_ref[0:n_keys]
        vb = vb_ref[0:n_keys]
        for g in range(group):
            q = q_ref[0, :, g * hd:(g + 1) * hd]
            qb = (_rope(_rms(q, qn_ref[...]), cos, sin) * scale).astype(BF16)
            p, denom = _softmax_rows(_dot_nt(qb, kb))
            o_ref[0, :, g * hd:(g + 1) * hd] = (_dot(p, vb) / denom).astype(o_ref.dtype)

    @pl.when(t * tq < ctx_len)
    def _():
        attend(ctx_len)

    @pl.when(t * tq >= ctx_len)
    def _():
        attend(s_all)


def _rope_tables(n_lat, ctx_len):
    pos = np.arange(n_lat)
    row = (pos // GRID_W).astype(np.float32)
    col = (pos % GRID_W).astype(np.float32)
    half = GQA_HEAD_DIM // 2
    inv_freq = (1.0 / (ROPE_THETA ** (np.arange(0, half, 2, dtype=np.float32) / half))).astype(np.float32)
    ar = row[:, None] * inv_freq
    ac = col[:, None] * inv_freq
    cos = np.concatenate([np.cos(ar), np.cos(ar), np.cos(ac), np.cos(ac)], axis=1)
    sin = np.concatenate([-np.sin(ar), np.sin(ar), -np.sin(ac), np.sin(ac)], axis=1)
    cos = np.concatenate([np.ones((ctx_len, GQA_HEAD_DIM)), cos], axis=0).astype(np.float32)
    sin = np.concatenate([np.zeros((ctx_len, GQA_HEAD_DIM)), sin], axis=0).astype(np.float32)
    return jnp.asarray(cos), jnp.asarray(sin)


def _gqa_attention(qkv, q_norm, k_norm, ctx_len):
    b, s, n = qkv.shape
    hd = GQA_HEAD_DIM
    n_q = n - 2 * GQA_KV_HEADS * hd
    group = n_q // hd // GQA_KV_HEADS
    cos, sin = _rope_tables(s - ctx_len, ctx_len)
    tq = ROW_TILE
    q_blocks = n_q // hd
    return pl.pallas_call(
        functools.partial(_gqa_kernel, ctx_len=ctx_len, group=group),
        grid=(b, GQA_KV_HEADS, s // tq),
        in_specs=[
            pl.BlockSpec((1, tq, group * hd), lambda i, h, t: (i, t, h)),
            pl.BlockSpec((1, s, hd), lambda i, h, t: (i, 0, q_blocks + h)),
            pl.BlockSpec((1, s, hd), lambda i, h, t: (i, 0, q_blocks + GQA_KV_HEADS + h)),
            pl.BlockSpec((tq, hd), lambda i, h, t: (t, 0)),
            pl.BlockSpec((tq, hd), lambda i, h, t: (t, 0)),
            _const_spec((s, hd)),
            _const_spec((s, hd)),
            _const_spec((1, hd)),
            _const_spec((1, hd)),
        ],
        out_specs=pl.BlockSpec((1, tq, group * hd), lambda i, h, t: (i, t, h)),
        out_shape=jax.ShapeDtypeStruct((b, s, n_q), BF16),
        scratch_shapes=[pltpu.VMEM((s, hd), BF16), pltpu.VMEM((s, hd), BF16)],
        compiler_params=_params("parallel", "parallel", "arbitrary"),
        name="gqa_attention",
    )(qkv, qkv, qkv, cos, sin, cos, sin, q_norm.reshape(1, hd), k_norm.reshape(1, hd))


def _s5_kernel(u_ref, perm_ref, m_ref, p_ref, q_ref, a_ref, y_ref, s_ref, h_ref, yg_ref, *, n_chunks, ctx_chunks, nb):
    w = m_ref.shape[1]
    lhs = u_ref[0]
    for g in range(m_ref.shape[0]):
        cols = slice(g * w, (g + 1) * w)
        u = _dot(lhs, perm_ref[:, cols]).astype(BF16)
        y = _s5_group(u, m_ref[g], p_ref[g], q_ref[g], a_ref[g], s_ref, h_ref, n_chunks, ctx_chunks, nb)
        yg_ref[:, cols] = y.astype(BF16)
    yg = yg_ref[...]
    for blk in range(m_ref.shape[0]):
        rows = slice(blk * w, (blk + 1) * w)
        y_ref[0, :, rows] = _dot_nt(yg, perm_ref[rows, :]).astype(y_ref.dtype)


def _s5_group(u, m, p, q, a, s_ref, h_ref, n_chunks, ctx_chunks, nb):
    half = S5_STATE
    s_ref[...] = _dot(u, p)
    a_re = a[0:1]
    a_im = a[1:2]
    is_fwd = lax.broadcasted_iota(jnp.int32, (nb, 2 * half), 1) < half

    def step(t, carry):
        h_re, h_im = carry
        k_rev = jnp.where(t < ctx_chunks, ctx_chunks - 1 - t, n_chunks - 1 - (t - ctx_chunks))
        rf = pl.multiple_of(t * nb, nb)
        rr = pl.multiple_of(k_rev * nb, nb)
        h_ref[pl.ds(rf, nb), 0:half] = h_re[:, 0:half]
        h_ref[pl.ds(rr, nb), half:2 * half] = h_re[:, half:2 * half]
        h_ref[pl.ds(rf, nb), 2 * half:3 * half] = h_im[:, 0:half]
        h_ref[pl.ds(rr, nb), 3 * half:4 * half] = h_im[:, half:2 * half]
        s_re = jnp.where(is_fwd, s_ref[pl.ds(rf, nb), 0:2 * half], s_ref[pl.ds(rr, nb), 0:2 * half])
        s_im = jnp.where(is_fwd, s_ref[pl.ds(rf, nb), 2 * half:4 * half], s_ref[pl.ds(rr, nb), 2 * half:4 * half])
        return a_re * h_re - a_im * h_im + s_re, a_re * h_im + a_im * h_re + s_im

    zero = jnp.zeros((nb, 2 * half), F32)
    lax.fori_loop(0, n_chunks, step, (zero, zero))
    return _dot(u, m) + _dot(h_ref[...].astype(BF16), q)


def _s5_tables(a_re, a_im, log_dt, b_re, b_im, c_re, c_im, d_skip):
    tc = S5_CHUNK
    n_groups = a_re.shape[1]
    lam = lax.complex(a_re.astype(F32), a_im.astype(F32))
    dt = jnp.exp(log_dt.astype(F32))[..., None]
    a_bar = jnp.exp(lam * dt)
    b_bar = ((a_bar - 1.0) / lam)[..., None] * lax.complex(b_re.astype(F32), b_im.astype(F32))
    cc = lax.complex(c_re.astype(F32), c_im.astype(F32))
    steps = jnp.arange(tc + 1, dtype=F32)
    apow = jnp.exp((lam * dt)[None] * steps[:, None, None, None])
    kern = jnp.einsum("dgcp,ldgp,dgpe->dlgce", cc, apow[:tc], b_bar).real
    jj = np.arange(tc)[:, None]
    ii = np.arange(tc)[None, :]
    kf = kern[0][np.clip(ii - jj, 0, tc - 1)] * jnp.asarray(ii >= jj, F32)[:, :, None, None, None]
    kr = kern[1][np.clip(jj - ii, 0, tc - 1)] * jnp.asarray(jj >= ii, F32)[:, :, None, None, None]
    m = jnp.transpose(kf + kr, (2, 0, 4, 1, 3))
    skip = d_skip.astype(F32).reshape(n_groups, S5_GROUP)
    eye_t = jnp.eye(tc, dtype=F32)
    eye_c = jnp.eye(S5_GROUP, dtype=F32)
    m = m + skip[:, None, :, None, None] * eye_t[None, :, None, :, None] * eye_c[None, None, :, None, :]
    m = m.reshape(n_groups, tc * S5_GROUP, tc * S5_GROUP)
    pf = jnp.einsum("jgp,gpe->gjep", apow[:tc][::-1, 0], b_bar[0])
    pr = jnp.einsum("jgp,gpe->gjep", apow[:tc, 1], b_bar[1])
    p = jnp.concatenate([pf.real, pr.real, pf.imag, pr.imag], axis=-1).reshape(n_groups, tc * S5_GROUP, 4 * S5_STATE)
    wf = jnp.einsum("gcp,igp->gpic", cc[0], apow[1:, 0])
    wr = jnp.einsum("gcp,igp->gpic", cc[1], apow[1:][::-1, 1])
    q = jnp.concatenate([wf.real, wr.real, -wf.imag, -wr.imag], axis=1).reshape(n_groups, 4 * S5_STATE, tc * S5_GROUP)
    a_tc = apow[tc]
    dec = jnp.stack([jnp.concatenate([a_tc[0].real, a_tc[1].real], axis=-1),
                     jnp.concatenate([a_tc[0].imag, a_tc[1].imag], axis=-1)], axis=1)
    dec = jnp.pad(dec, ((0, 0), (0, 6), (0, 0)))
    return m.astype(BF16), p.astype(BF16), q.astype(BF16), dec


def _s5_core(h, tables, ctx_len):
    b, s, d = h.shape
    tc = S5_CHUNK
    n_groups = d // S5_GROUP
    n_chunks = s // tc
    w = tc * S5_GROUP
    m, p, q, dec = tables
    slab = 128
    n_slabs = d // slab
    n_sub = slab // S5_GROUP
    rows = n_chunks * b
    u = h.reshape(b, n_chunks, tc, n_slabs, slab).transpose(3, 1, 0, 2, 4).reshape(n_slabs, rows, tc * slab)
    src = np.arange(tc * slab)
    dst = ((src % slab) // S5_GROUP) * w + (src // slab) * S5_GROUP + src % S5_GROUP
    perm = (jnp.asarray(dst)[:, None] == jnp.arange(tc * slab)[None, :]).astype(BF16)
    blk = lambda o: (o, 0, 0)
    y = pl.pallas_call(
        functools.partial(_s5_kernel, n_chunks=n_chunks, ctx_chunks=ctx_len // tc, nb=b),
        grid=(n_slabs,),
        in_specs=[
            pl.BlockSpec((1, rows, tc * slab), blk),
            _const_spec(perm.shape),
            pl.BlockSpec((n_sub, w, w), blk),
            pl.BlockSpec((n_sub, w, 4 * S5_STATE), blk),
            pl.BlockSpec((n_sub, 4 * S5_STATE, w), blk),
            pl.BlockSpec((n_sub, 8, 2 * S5_STATE), blk),
        ],
        out_specs=pl.BlockSpec((1, rows, tc * slab), blk),
        out_shape=jax.ShapeDtypeStruct((n_slabs, rows, tc * slab), BF16),
        scratch_shapes=[pltpu.VMEM((rows, 4 * S5_STATE), F32), pltpu.VMEM((rows, 4 * S5_STATE), F32),
                        pltpu.VMEM((rows, tc * slab), BF16)],
        compiler_params=_params("parallel"),
        name="s5_core",
    )(u, perm, m, p, q, dec)
    return y.reshape(n_slabs, n_chunks, b, tc, slab).transpose(2, 1, 3, 0, 4).reshape(b, s, d)


def _glu_res_kernel(y_ref, x_ref, md_ref, g_ref, w_ref, b_ref, o_ref):
    y = y_ref[0].astype(F32)
    gel = y * (0.5 * (1.0 + jnp.tanh(np.sqrt(2.0 / np.pi).astype(np.float32) * (y + 0.044715 * (y * y * y)))))
    out = gel * _sigmoid(_dot(gel.astype(BF16), w_ref[...]) + b_ref[...])
    o_ref[0] = x_ref[0] + md_ref[0, 0][2:3] * _rms(out, g_ref[1:2])


def _glu_res(y, xs, md, g, w, bias, ctx_tiles=1):
    b, s, d = xs.shape
    return pl.pallas_call(
        _glu_res_kernel,
        grid=(b, s // ROW_TILE),
        in_specs=[_row_spec(d), _row_spec(d), _mod_spec(d, ctx_tiles), _const_spec(g.shape),
                  _const_spec(w.shape), _const_spec((1, d))],
        out_specs=_row_spec(d),
        out_shape=jax.ShapeDtypeStruct((b, s, d), F32),
        compiler_params=_params("parallel", "parallel"),
        name="glu_res",
    )(y, xs, md, g, w, bias.reshape(1, d))


def _conv_silu_kernel(x_ref, prev_ref, next_ref, w_ref, b_ref, o_ref, *, ctx_len, seq_len, width):
    t = pl.program_id(1)
    tm = x_ref.shape[1]
    halo = prev_ref.shape[1]
    x = x_ref[0].astype(F32)
    lo = t * tm
    hi = lo + tm
    prev_ok = jnp.where((lo == 0) | (lo == ctx_len), 0.0, 1.0)
    next_ok = jnp.where((hi == ctx_len) | (hi == seq_len), 0.0, 1.0)
    prev = prev_ref[0].astype(F32) * prev_ok
    nxt = next_ref[0].astype(F32) * next_ok
    row = lax.broadcasted_iota(jnp.int32, x.shape, 0)
    half = width // 2
    acc = x * w_ref[half:half + 1] + b_ref[...]
    for k in range(1, half + 1):
        back = pltpu.roll(x, k, 0)
        for r in range(k):
            back = jnp.where(row == r, prev[halo - k + r:halo - k + r + 1], back)
        acc = acc + back * w_ref[half - k:half - k + 1]
        fwd = pltpu.roll(x, tm - k, 0)
        for r in range(k):
            fwd = jnp.where(row == tm - k + r, nxt[r:r + 1], fwd)
        acc = acc + fwd * w_ref[half + k:half + k + 1]
    o_ref[0] = (acc * _sigmoid(acc)).astype(o_ref.dtype)


def _conv_silu(x, w, bias, ctx_len, width):
    b, s, n = x.shape
    tm = ROW_TILE
    halo = 16
    sub = tm // halo
    last = s // halo - 1
    return pl.pallas_call(
        functools.partial(_conv_silu_kernel, ctx_len=ctx_len, seq_len=s, width=width),
        grid=(b, s // tm),
        in_specs=[
            _row_spec(n),
            pl.BlockSpec((1, halo, n), lambda i, t: (i, jnp.maximum(t * sub - 1, 0), 0)),
            pl.BlockSpec((1, halo, n), lambda i, t: (i, jnp.minimum((t + 1) * sub, last), 0)),
            _const_spec(w.shape),
            _const_spec((1, n)),
        ],
        out_specs=_row_spec(n),
        out_shape=jax.ShapeDtypeStruct((b, s, n), BF16),
        compiler_params=_params("parallel", "parallel"),
        name="conv_silu",
    )(x, x, x, w, bias.reshape(1, n))


def _softplus(z):
    return jnp.maximum(z, 0.0) + jnp.log1p(jnp.exp(-jnp.abs(z)))


def _expand_heads(v, e3):
    hi = v.astype(BF16)
    r1 = v - hi.astype(F32)
    mid = r1.astype(BF16)
    lo = (r1 - mid.astype(F32)).astype(BF16)
    return _dot(jnp.concatenate([hi, mid, lo], axis=1), e3)


def _ssd_kernel(*refs, rev, n_heads):
    if rev:
        (xs_ref, bm_ref, cm_ref, dtc_ref, dtr_ref, pr_ref, pc_ref, yprev_ref, dsk_ref, y_ref, st_ref) = refs
    else:
        (xs_ref, bm_ref, cm_ref, dtc_ref, dtr_ref, pr_ref, pc_ref, y_ref, st_ref) = refs
    t = pl.program_id(1)
    tc = xs_ref.shape[1]
    hd = SSD_HEAD_DIM
    pair = 2 * hd
    heads_per_group = n_heads // SSD_GROUPS

    @pl.when(t == 0)
    def _():
        st_ref[...] = jnp.zeros(st_ref.shape, F32)

    dtc = _softplus(dtc_ref[0, 0] + pr_ref[0:1])
    dtr = _softplus(dtr_ref[0, 0] + pc_ref[:, 0:1])
    dac = dtc * -jnp.exp(pr_ref[1:2])
    dar = dtr * -jnp.exp(pc_ref[:, 1:2])
    ii = lax.broadcasted_iota(jnp.int32, (tc, tc), 0)
    jj = lax.broadcasted_iota(jnp.int32, (tc, tc), 1)
    visible = (jj >= ii) if rev else (ii >= jj)
    tri = jnp.where(visible, 1.0, 0.0)
    hp = lax.Precision.HIGHEST
    cum_c = jnp.dot(tri, dac, precision=hp, preferred_element_type=F32)
    cum_r = lax.dot_general(dar, tri, (((1,), (1,)), ((), ())), precision=hp, preferred_element_type=F32)
    tot = cum_c[0:1] if rev else cum_c[tc - 1:tc]

    e_row = lax.broadcasted_iota(jnp.int32, (3 * n_heads, n_heads * hd), 0)
    e_col = lax.broadcasted_iota(jnp.int32, (3 * n_heads, n_heads * hd), 1)
    e3 = jnp.where((e_row % n_heads) == (e_col // hd), 1.0, 0.0).astype(BF16)
    w_out = _expand_heads(dtc * jnp.exp(tot - cum_c), e3)
    w_in = _expand_heads(jnp.exp(cum_c), e3)
    dec = _expand_heads(jnp.broadcast_to(jnp.exp(tot), (8, n_heads)), e3)[0:1]

    xs = xs_ref[0]
    lane = lax.broadcasted_iota(jnp.int32, (tc, pair), 1)
    first = lane < hd
    for g in range(SSD_GROUPS):
        bmb = bm_ref[0, :, g * SSD_STATE:(g + 1) * SSD_STATE].astype(BF16)
        cmb = cm_ref[0, :, g * SSD_STATE:(g + 1) * SSD_STATE].astype(BF16)
        scores = _dot_nt(cmb, bmb)
        for k in range(heads_per_group // 2):
            hp_i = g * (heads_per_group // 2) + k
            cols = slice(hp_i * pair, (hp_i + 1) * pair)
            x2b = xs[:, cols]
            x2 = x2b.astype(F32)
            halves = []
            for s_i in range(2):
                h = 2 * hp_i + s_i
                decay = jnp.exp(jnp.where(visible, cum_c[:, h:h + 1] - cum_r[h:h + 1, :], -jnp.inf))
                wgt = (scores * decay * dtr[h:h + 1, :]).astype(BF16)
                halves.append(_dot(wgt, x2b))
            y_diag = jnp.where(first, halves[0], halves[1])
            st = st_ref[hp_i]
            y_off = _dot(cmb, st.astype(BF16)) * w_in[:, cols]
            y = y_diag + y_off
            if rev:
                y = y + yprev_ref[0, :, cols].astype(F32) + x2 * dsk_ref[:, cols]
            y_ref[0, :, cols] = y.astype(y_ref.dtype)
            st_ref[hp_i] = dec[:, cols] * st + _dot_tn(bmb, (x2 * w_out[:, cols]).astype(BF16))


def _ssd_scan(xbc, dt_cols, dt_rows, prm_rows, prm_cols, d_inner, ctx_len, rev, y_prev=None, d_skip=None):
    b, s, _ = xbc.shape
    tc = SSD_CHUNK
    n_heads = d_inner // SSD_HEAD_DIM
    nbc = SSD_GROUPS * SSD_STATE
    n_chunks = s // tc
    ctx_chunks = ctx_len // tc
    d = 1 if rev else 0
    if rev:
        chunk = lambda t: jnp.where(t < ctx_chunks, ctx_chunks - 1 - t, n_chunks - 1 - (t - ctx_chunks))
    else:
        chunk = lambda t: t
    in_specs = [
        pl.BlockSpec((1, tc, d_inner), lambda i, t: (i, chunk(t), 0)),
        pl.BlockSpec((1, tc, nbc), lambda i, t: (i, chunk(t), d_inner // nbc)),
        pl.BlockSpec((1, tc, nbc), lambda i, t: (i, chunk(t), d_inner // nbc + 1)),
        pl.BlockSpec((1, 1, tc, n_heads), lambda i, t: (i, d, chunk(t), 0)),
        pl.BlockSpec((1, 1, n_heads, tc), lambda i, t: (i, d, 0, chunk(t))),
        _const_spec(prm_rows.shape),
        _const_spec(prm_cols.shape),
    ]
    args = [xbc, xbc, xbc, dt_cols, dt_rows, prm_rows, prm_cols]
    if rev:
        in_specs += [pl.BlockSpec((1, tc, d_inner), lambda i, t: (i, chunk(t), 0)), _const_spec((1, d_inner))]
        args += [y_prev, d_skip]
    return pl.pallas_call(
        functools.partial(_ssd_kernel, rev=rev, n_heads=n_heads),
        grid=(b, n_chunks),
        in_specs=in_specs,
        out_specs=pl.BlockSpec((1, tc, d_inner), lambda i, t: (i, chunk(t), 0)),
        out_shape=jax.ShapeDtypeStruct((b, s, d_inner), BF16),
        scratch_shapes=[pltpu.VMEM((n_heads // 2, SSD_STATE, 2 * SSD_HEAD_DIM), F32)],
        compiler_params=_params("parallel", "arbitrary"),
        name="ssd_rev" if rev else "ssd_fwd",
    )(*args)


def _ssd_out_kernel(y_ref, gt_ref, x_ref, md_ref, g_ref, ng_ref, w_ref, o_ref):
    gt = gt_ref[0].astype(F32)
    yg = y_ref[0].astype(F32) * (gt * _sigmoid(gt))
    width = yg.shape[1] // SSD_GROUPS
    r = jnp.zeros(x_ref.shape[1:], F32)
    for k in range(SSD_GROUPS):
        seg = _rms(yg[:, k * width:(k + 1) * width], ng_ref[:, k * width:(k + 1) * width])
        r = r + _dot(seg.astype(BF16), w_ref[k * width:(k + 1) * width, :])
    o_ref[0] = x_ref[0] + md_ref[0, 0][2:3] * _rms(r, g_ref[1:2])


def _ssd_out(y, gate, xs, md, g, norm_g, w, ctx_tiles=1):
    b, s, d = xs.shape
    n = y.shape[-1]
    return pl.pallas_call(
        _ssd_out_kernel,
        grid=(b, s // ROW_TILE),
        in_specs=[_row_spec(n), _row_spec(n), _row_spec(d), _mod_spec(d, ctx_tiles), _const_spec(g.shape),
                  _const_spec((1, n)), _const_spec(w.shape)],
        out_specs=_row_spec(d),
        out_shape=jax.ShapeDtypeStruct((b, s, d), F32),
        compiler_params=_params("parallel", "parallel"),
        name="ssd_out",
    )(y, gate, xs, md, g, norm_g.reshape(1, n), w)


def _na_kernel(q_ref, k_ref, v_ref, bias_ref, o_ref, *, ctx_len, grid_rows):
    r = pl.program_id(1)
    hd = NA_HEAD_DIM
    pair = 2 * hd
    n_heads = q_ref.shape[2] // hd
    win = NA_WIN_ROWS * GRID_W
    r0 = jnp.clip(r - NA_WIN_ROWS // 2, 0, grid_rows - NA_WIN_ROWS)
    base = pl.multiple_of(ctx_len + r0 * GRID_W, GRID_W)
    lane = lax.broadcasted_iota(jnp.int32, (GRID_W, pair), 1)
    first = lane < hd
    scale = hd ** -0.5
    s_loc, s_ctx = [], []
    for h in range(n_heads):
        cols = slice((h // 2) * pair, (h // 2 + 1) * pair)
        qm = jnp.where(first if h % 2 == 0 else ~first, q_ref[0, :, cols] * scale, 0.0).astype(BF16)
        s_loc.append(_dot_nt(qm, k_ref[0, pl.ds(base, win), cols]) + bias_ref[0, h])
        s_ctx.append(_dot_nt(qm, k_ref[0, 0:ctx_len, cols]))
    p_loc, p_ctx, inv = [], [], []
    for h in range(n_heads):
        m = jnp.maximum(jnp.max(s_loc[h], axis=-1, keepdims=True), jnp.max(s_ctx[h], axis=-1, keepdims=True))
        e_loc = jnp.exp(s_loc[h] - m)
        e_ctx = jnp.exp(s_ctx[h] - m)
        inv.append(1.0 / (jnp.sum(e_loc, axis=-1, keepdims=True) + jnp.sum(e_ctx, axis=-1, keepdims=True)))
        p_loc.append(e_loc.astype(BF16))
        p_ctx.append(e_ctx.astype(BF16))
    for hp_i in range(n_heads // 2):
        cols = slice(hp_i * pair, (hp_i + 1) * pair)
        vw = v_ref[0, pl.ds(base, win), cols]
        vc = v_ref[0, 0:ctx_len, cols]
        halves = [(_dot(p_loc[h], vw) + _dot(p_ctx[h], vc)) * inv[h] for h in (2 * hp_i, 2 * hp_i + 1)]
        o_ref[0, :, cols] = jnp.where(first, halves[0], halves[1]).astype(o_ref.dtype)


def _na_bias_table(rpb):
    n_heads, n_rr, n_cr = rpb.shape
    qc = np.arange(GRID_W)[:, None]
    kc = np.arange(GRID_W)[None, :]
    start = np.clip(qc - NA_WIN_COLS // 2, 0, GRID_W - NA_WIN_COLS)
    inside = (kc >= start) & (kc < start + NA_WIN_COLS)
    col_rel = kc - qc + (NA_WIN_COLS - 1)
    onehot = ((col_rel[None] == np.arange(n_cr)[:, None, None]) & inside[None]).astype(np.float32)
    t2 = jnp.dot(rpb.astype(F32).reshape(n_heads * n_rr, n_cr), jnp.asarray(onehot.reshape(n_cr, GRID_W * GRID_W)),
                 precision=lax.Precision.HIGHEST).reshape(n_heads, n_rr, GRID_W, GRID_W)
    t2 = jnp.where(jnp.asarray(inside), t2, NEG_BIG)
    tbl = jnp.stack([t2[:, NA_WIN_ROWS - 1 - dr:2 * NA_WIN_ROWS - 1 - dr] for dr in range(NA_WIN_ROWS)])
    return jnp.transpose(tbl, (0, 1, 3, 2, 4)).reshape(NA_WIN_ROWS, n_heads, GRID_W, NA_WIN_ROWS * GRID_W)


def _na_attention(q, k, v, bias, ctx_len):
    b, s, d = q.shape
    n_lat = s - ctx_len
    grid_rows = n_lat // GRID_W
    n_heads = d // NA_HEAD_DIM
    q_off = ctx_len // GRID_W

    def bias_index(i, r):
        return (r - jnp.clip(r - NA_WIN_ROWS // 2, 0, grid_rows - NA_WIN_ROWS), 0, 0, 0)

    return pl.pallas_call(
        functools.partial(_na_kernel, ctx_len=ctx_len, grid_rows=grid_rows),
        grid=(b, grid_rows),
        in_specs=[
            pl.BlockSpec((1, GRID_W, d), lambda i, r: (i, q_off + r, 0)),
            pl.BlockSpec((1, s, d), lambda i, r: (i, 0, 0)),
            pl.BlockSpec((1, s, d), lambda i, r: (i, 0, 0)),
            pl.BlockSpec((1, n_heads, GRID_W, NA_WIN_ROWS * GRID_W), bias_index),
        ],
        out_specs=pl.BlockSpec((1, GRID_W, d), lambda i, r: (i, r, 0)),
        out_shape=jax.ShapeDtypeStruct((b, n_lat, d), BF16),
        compiler_params=_params("parallel", "arbitrary"),
        name="na_attention",
    )(q, k, v, bias)


def _forward(x, c, ctx, c_ctx, ada_w, ada_b, norm_g, mlp_w1, mlp_w2,
             gqa_w_qkv, gqa_q_norm, gqa_k_norm, gqa_w_o,
             s5_a_re, s5_a_im, s5_log_dt, s5_b_re, s5_b_im, s5_c_re, s5_c_im, s5_d, s5_glu_w, s5_glu_b,
             ssd_w_in, ssd_conv_w, ssd_conv_b, ssd_dt_bias, ssd_a_log, ssd_d, ssd_norm_g, ssd_w_out,
             na_w_qkv, na_rpb, na_w_o):
    bsz, n_lat, d = x.shape
    ctx_len = ctx.shape[1]
    depth = ada_w.shape[0]
    n_mod = ada_w.shape[2] // d
    assert depth == 4 and ctx_len == ROW_TILE and n_lat % ROW_TILE == 0 and n_lat % GRID_W == 0
    assert n_lat // GRID_W >= NA_WIN_ROWS

    cond = jnp.concatenate([c, c_ctx[None]], axis=0)
    cond = jnp.pad(cond, ((0, -cond.shape[0] % 8), (0, 0)))
    ada = _ada(cond, ada_w, ada_b)

    xs = jnp.concatenate([ctx, x], axis=1)
    streams = []
    for i in range(depth):
        kind = i % 4
        j = i // 4
        mod = ada[i, :bsz].reshape(bsz, n_mod, d)
        modc = jnp.broadcast_to(ada[i, bsz].reshape(1, n_mod, d), (bsz, n_mod, d))
        md = jnp.pad(jnp.stack([modc, mod], axis=1), ((0, 0), (0, 0), (0, 8 - n_mod), (0, 0)))
        g = jnp.pad(norm_g[i], ((0, 4), (0, 0)))
        w1 = mlp_w1[i].astype(BF16)
        w2 = mlp_w2[i].astype(BF16)
        ctx_tiles = 1
        if kind == 0:
            w_qkv = gqa_w_qkv[j].astype(BF16)
            n = w_qkv.shape[1]
            (qkv,) = _nm_matmul(xs, md, g, w_qkv, ((0, n),), (F32,))
            o = _gqa_attention(qkv, gqa_q_norm[j], gqa_k_norm[j], ctx_len)
            xs = _proj_res(o, xs, md, g, gqa_w_o[j].astype(BF16))
        elif kind == 1:
            h = _nm(xs, md, g, BF16)
            tables = _s5_tables(s5_a_re[j], s5_a_im[j], s5_log_dt[j], s5_b_re[j], s5_b_im[j],
                                s5_c_re[j], s5_c_im[j], s5_d[j])
            y = _s5_core(h, tables, ctx_len)
            xs = _glu_res(y, xs, md, g, s5_glu_w[j].astype(BF16), s5_glu_b[j])
        elif kind == 2:
            w_in = ssd_w_in[j].astype(BF16)
            d_inner = ssd_w_out.shape[1]
            n_heads = d_inner // SSD_HEAD_DIM
            conv_ch = ssd_conv_w.shape[2]
            splits = ((0, d_inner), (d_inner, d_inner + conv_ch), (d_inner + conv_ch, w_in.shape[1]))
            gate, pre, dt_raw = _nm_matmul(xs, md, g, w_in, splits, (BF16, BF16, F32))
            conv_w = jnp.pad(ssd_conv_w[j], ((0, 8 - ssd_conv_w.shape[1]), (0, 0)))
            xbc = _conv_silu(pre, conv_w, ssd_conv_b[j], ctx_len, ssd_conv_w.shape[1])
            dt_cols = dt_raw.reshape(bsz, -1, 2, n_heads).transpose(0, 2, 1, 3)
            dt_rows = dt_cols.transpose(0, 1, 3, 2)
            d_skip = jnp.repeat(ssd_d[j].astype(F32), SSD_HEAD_DIM).reshape(1, d_inner)
            y = None
            for direction in (0, 1):
                prm = jnp.stack([ssd_dt_bias[j, direction], ssd_a_log[j, direction]]).astype(F32)
                prm_rows = jnp.pad(prm, ((0, 6), (0, 0)))
                prm_cols = jnp.pad(prm.T, ((0, 0), (0, 6)))
                y = _ssd_scan(xbc, dt_cols, dt_rows, prm_rows, prm_cols, d_inner, ctx_len, direction == 1,
                              y_prev=y, d_skip=d_skip)
            xs = _ssd_out(y, gate, xs, md, g, ssd_norm_g[j], ssd_w_out[j].astype(BF16))
        else:
            w_qkv = na_w_qkv[j].astype(BF16)
            q, k, v = _nm_matmul(xs, md, g, w_qkv, ((0, d), (d, 2 * d), (2 * d, 3 * d)), (F32, BF16, BF16))
            o = _na_attention(q, k, v, _na_bias_table(na_rpb[j]), ctx_len)
            xs = xs[:, ctx_len:]
            ctx_tiles = 0
            xs = _proj_res(o, xs, md, g, na_w_o[j].astype(BF16), ctx_tiles=ctx_tiles)
        xs = _mlp(xs, md, g, w1, w2, ctx_tiles=ctx_tiles)
        streams.append(xs)
    return streams


def kernel(x, c, ctx, c_ctx, ada_w, ada_b, norm_g, mlp_w1, mlp_w2, gqa_w_qkv, gqa_q_norm, gqa_k_norm, gqa_w_o, s5_a_re, s5_a_im, s5_log_dt, s5_b_re, s5_b_im, s5_c_re, s5_c_im, s5_d, s5_glu_w, s5_glu_b, ssd_w_in, ssd_conv_w, ssd_conv_b, ssd_dt_bias, ssd_a_log, ssd_d, ssd_norm_g, ssd_w_out, na_w_qkv, na_rpb, na_w_o):
    return _forward(x, c, ctx, c_ctx, ada_w, ada_b, norm_g, mlp_w1, mlp_w2, gqa_w_qkv, gqa_q_norm, gqa_k_norm,
                    gqa_w_o, s5_a_re, s5_a_im, s5_log_dt, s5_b_re, s5_b_im, s5_c_re, s5_c_im, s5_d, s5_glu_w,
                    s5_glu_b, ssd_w_in, ssd_conv_w, ssd_conv_b, ssd_dt_bias, ssd_a_log, ssd_d, ssd_norm_g,
                    ssd_w_out, na_w_qkv, na_rpb, na_w_o)[-1]
```

```python
import functools

import numpy as np
import jax
import jax.numpy as jnp
from jax import lax
from jax.experimental import pallas as pl
from jax.experimental.pallas import tpu as pltpu

F32 = jnp.float32
BF16 = jnp.bfloat16

NORM_EPS = 1e-6
GRID_W = 64
ROW_TILE = 256
VMEM_LIMIT_BYTES = 56 * 1024 * 1024
NEG_BIG = -1e30

GQA_HEAD_DIM = 128
GQA_KV_HEADS = 2
ROPE_THETA = 10000.0
S5_GROUP = 16
S5_STATE = 64
S5_CHUNK = 16
SSD_HEAD_DIM = 64
SSD_GROUPS = 4
SSD_STATE = 128
SSD_CHUNK = 128
NA_HEAD_DIM = 64
NA_WIN_ROWS = 8
NA_WIN_COLS = 16


def _params(*semantics):
    return pltpu.CompilerParams(dimension_semantics=semantics, vmem_limit_bytes=VMEM_LIMIT_BYTES)


def _rms(y, g):
    return y * lax.rsqrt(jnp.mean(y * y, axis=-1, keepdims=True) + NORM_EPS) * g


def _dot(a, b):
    return jnp.dot(a, b, preferred_element_type=F32)


def _dot_nt(a, b):
    return lax.dot_general(a, b, (((1,), (1,)), ((), ())), preferred_element_type=F32)


def _dot_tn(a, b):
    return lax.dot_general(a, b, (((0,), (0,)), ((), ())), preferred_element_type=F32)


def _sigmoid(z):
    return 1.0 / (1.0 + jnp.exp(-z))


def _const_spec(shape):
    nd = len(shape)
    return pl.BlockSpec(shape, lambda *_: (0,) * nd)


def _mod_spec(d, ctx_tiles):
    return pl.BlockSpec((1, 1, 8, d), lambda b, t: (b, jnp.where(t < ctx_tiles, 0, 1), 0, 0))


def _row_spec(n, tm=ROW_TILE):
    return pl.BlockSpec((1, tm, n), lambda b, t: (b, t, 0))


def _ada_kernel(c_ref, w_ref, b_ref, o_ref):
    cc = c_ref[...]
    s = cc * _sigmoid(cc)
    o_ref[0] = _dot(s.astype(BF16), w_ref[0].astype(BF16)) + b_ref[0]


def _ada(cond, ada_w, ada_b):
    depth, d, n = ada_w.shape
    rows = cond.shape[0]
    tn = 1536
    return pl.pallas_call(
        _ada_kernel,
        grid=(depth, n // tn),
        in_specs=[
            pl.BlockSpec((rows, d), lambda i, j: (0, 0)),
            pl.BlockSpec((1, d, tn), lambda i, j: (i, 0, j)),
            pl.BlockSpec((1, 1, tn), lambda i, j: (i, 0, j)),
        ],
        out_specs=pl.BlockSpec((1, rows, tn), lambda i, j: (i, 0, j)),
        out_shape=jax.ShapeDtypeStruct((depth, rows, n), F32),
        compiler_params=_params("parallel", "parallel"),
        name="ada",
    )(cond, ada_w, ada_b.reshape(depth, 1, n))


def _nm_matmul_kernel(x_ref, md_ref, g_ref, w_ref, *o_refs, splits):
    md = md_ref[0, 0]
    h = _rms(x_ref[0], g_ref[0:1]) * (1.0 + md[1:2]) + md[0:1]
    hb = h.astype(BF16)
    for o_ref, (lo, hi) in zip(o_refs, splits):
        o_ref[0] = _dot(hb, w_ref[:, lo:hi]).astype(o_ref.dtype)


def _nm_matmul(xs, md, g, w, splits, dtypes, ctx_tiles=1):
    b, s, d = xs.shape
    out_shape = [jax.ShapeDtypeStruct((b, s, hi - lo), dt) for (lo, hi), dt in zip(splits, dtypes)]
    return pl.pallas_call(
        functools.partial(_nm_matmul_kernel, splits=splits),
        grid=(b, s // ROW_TILE),
        in_specs=[_row_spec(d), _mod_spec(d, ctx_tiles), _const_spec(g.shape), _const_spec(w.shape)],
        out_specs=[_row_spec(hi - lo) for lo, hi in splits],
        out_shape=out_shape,
        compiler_params=_params("parallel", "parallel"),
        name="nm_matmul",
    )(xs, md, g, w)


def _nm_kernel(x_ref, md_ref, g_ref, o_ref):
    md = md_ref[0, 0]
    h = _rms(x_ref[0], g_ref[0:1]) * (1.0 + md[1:2]) + md[0:1]
    o_ref[0] = h.astype(o_ref.dtype)


def _nm(xs, md, g, dtype, ctx_tiles=1):
    b, s, d = xs.shape
    return pl.pallas_call(
        _nm_kernel,
        grid=(b, s // ROW_TILE),
        in_specs=[_row_spec(d), _mod_spec(d, ctx_tiles), _const_spec(g.shape)],
        out_specs=_row_spec(d),
        out_shape=jax.ShapeDtypeStruct((b, s, d), dtype),
        compiler_params=_params("parallel", "parallel"),
        name="nm",
    )(xs, md, g)


def _mlp_kernel(x_ref, md_ref, g_ref, w1_ref, w2_ref, o_ref, *, ff_chunk):
    x = x_ref[0]
    md = md_ref[0, 0]
    hb = (_rms(x, g_ref[2:3]) * (1.0 + md[4:5]) + md[3:4]).astype(BF16)
    acc = jnp.zeros(x.shape, F32)
    for lo in range(0, w1_ref.shape[1], ff_chunk):
        a = _dot(hb, w1_ref[:, lo:lo + ff_chunk])
        a = jnp.square(jnp.maximum(a, 0.0)).astype(BF16)
        acc = acc + _dot(a, w2_ref[lo:lo + ff_chunk, :])
    o_ref[0] = x + md[5:6] * _rms(acc, g_ref[3:4])


def _mlp(xs, md, g, w1, w2, ctx_tiles=1):
    b, s, d = xs.shape
    return pl.pallas_call(
        functools.partial(_mlp_kernel, ff_chunk=1024),
        grid=(b, s // ROW_TILE),
        in_specs=[_row_spec(d), _mod_spec(d, ctx_tiles), _const_spec(g.shape),
                  _const_spec(w1.shape), _const_spec(w2.shape)],
        out_specs=_row_spec(d),
        out_shape=jax.ShapeDtypeStruct((b, s, d), F32),
        compiler_params=_params("parallel", "parallel"),
        name="mlp",
    )(xs, md, g, w1, w2)


def _proj_res_kernel(y_ref, x_ref, md_ref, g_ref, w_ref, o_ref):
    r = _dot(y_ref[0].astype(BF16), w_ref[...])
    o_ref[0] = x_ref[0] + md_ref[0, 0][2:3] * _rms(r, g_ref[1:2])


def _proj_res(y, xs, md, g, w, ctx_tiles=1, skip_tiles=0):
    b, _, d = xs.shape
    s = y.shape[1]
    return pl.pallas_call(
        _proj_res_kernel,
        grid=(b, s // ROW_TILE),
        in_specs=[_row_spec(y.shape[-1]), pl.BlockSpec((1, ROW_TILE, d), lambda i, t: (i, t + skip_tiles, 0)),
                  _mod_spec(d, ctx_tiles), _const_spec(g.shape), _const_spec(w.shape)],
        out_specs=_row_spec(d),
        out_shape=jax.ShapeDtypeStruct((b, s, d), F32),
        compiler_params=_params("parallel", "parallel"),
        name="proj_res",
    )(y, xs, md, g, w)


def _rope(x, cos, sin_signed):
    lane = lax.broadcasted_iota(jnp.int32, x.shape, 1)
    quarter = GQA_HEAD_DIM // 4
    partner = jnp.where((lane % (2 * quarter)) < quarter,
                        pltpu.roll(x, GQA_HEAD_DIM - quarter, 1), pltpu.roll(x, quarter, 1))
    return x * cos + partner * sin_signed


SOFTMAX_ROW_BLOCK = 16


def _softmax_rows(s):
    ps, sums = [], []
    for lo in range(0, s.shape[0], SOFTMAX_ROW_BLOCK):
        sb = s[lo:lo + SOFTMAX_ROW_BLOCK]
        e = jnp.exp(sb - jnp.max(sb, axis=-1, keepdims=True))
        sums.append(jnp.sum(e, axis=-1, keepdims=True))
        ps.append(e.astype(BF16))
    return jnp.concatenate(ps, axis=0), jnp.concatenate(sums, axis=0)


def _gqa_kernel(q_ref, k_ref, v_ref, cq_ref, sq_ref, ck_ref, sk_ref, qn_ref, kn_ref, o_ref, kb_ref, vb_ref,
                *, ctx_len, group):
    t = pl.program_id(2)
    tq = q_ref.shape[1]
    s_all = k_ref.shape[1]
    hd = GQA_HEAD_DIM

    @pl.when(t == 0)
    def _():
        kn = _rms(k_ref[0], kn_ref[...])
        kb_ref[...] = _rope(kn, ck_ref[...], sk_ref[...]).astype(BF16)
        vb_ref[...] = v_ref[0].astype(BF16)

    cos = cq_ref[...]
    sin = sq_ref[...]
    scale = hd ** -0.5

    def attend(n_keys):
        kb = kb_ref[0:n_keys]
        vb = vb_ref[0:n_keys]
        for g in range(group):
            q = q_ref[0, :, g * hd:(g + 1) * hd]
            qb = (_rope(_rms(q, qn_ref[...]), cos, sin) * scale).astype(BF16)
            p, denom = _softmax_rows(_dot_nt(qb, kb))
            o_ref[0, :, g * hd:(g + 1) * hd] = (_dot(p, vb) / denom).astype(o_ref.dtype)

    @pl.when(t * tq < ctx_len)
    def _():
        attend(ctx_len)

    @pl.when(t * tq >= ctx_len)
    def _():
        attend(s_all)


def _rope_tables(n_lat, ctx_len):
    pos = np.arange(n_lat)
    row = (pos // GRID_W).astype(np.float32)
    col = (pos % GRID_W).astype(np.float32)
    half = GQA_HEAD_DIM // 2
    inv_freq = (1.0 / (ROPE_THETA ** (np.arange(0, half, 2, dtype=np.float32) / half))).astype(np.float32)
    ar = row[:, None] * inv_freq
    ac = col[:, None] * inv_freq
    cos = np.concatenate([np.cos(ar), np.cos(ar), np.cos(ac), np.cos(ac)], axis=1)
    sin = np.concatenate([-np.sin(ar), np.sin(ar), -np.sin(ac), np.sin(ac)], axis=1)
    cos = np.concatenate([np.ones((ctx_len, GQA_HEAD_DIM)), cos], axis=0).astype(np.float32)
    sin = np.concatenate([np.zeros((ctx_len, GQA_HEAD_DIM)), sin], axis=0).astype(np.float32)
    return jnp.asarray(cos), jnp.asarray(sin)


def _gqa_attention(qkv, q_norm, k_norm, ctx_len):
    b, s, n = qkv.shape
    hd = GQA_HEAD_DIM
    n_q = n - 2 * GQA_KV_HEADS * hd
    group = n_q // hd // GQA_KV_HEADS
    cos, sin = _rope_tables(s - ctx_len, ctx_len)
    tq = ROW_TILE
    q_blocks = n_q // hd
    return pl.pallas_call(
        functools.partial(_gqa_kernel, ctx_len=ctx_len, group=group),
        grid=(b, GQA_KV_HEADS, s // tq),
        in_specs=[
            pl.BlockSpec((1, tq, group * hd), lambda i, h, t: (i, t, h)),
            pl.BlockSpec((1, s, hd), lambda i, h, t: (i, 0, q_blocks + h)),
            pl.BlockSpec((1, s, hd), lambda i, h, t: (i, 0, q_blocks + GQA_KV_HEADS + h)),
            pl.BlockSpec((tq, hd), lambda i, h, t: (t, 0)),
            pl.BlockSpec((tq, hd), lambda i, h, t: (t, 0)),
            _const_spec((s, hd)),
            _const_spec((s, hd)),
            _const_spec((1, hd)),
            _const_spec((1, hd)),
        ],
        out_specs=pl.BlockSpec((1, tq, group * hd), lambda i, h, t: (i, t, h)),
        out_shape=jax.ShapeDtypeStruct((b, s, n_q), BF16),
        scratch_shapes=[pltpu.VMEM((s, hd), BF16), pltpu.VMEM((s, hd), BF16)],
        compiler_params=_params("parallel", "parallel", "arbitrary"),
        name="gqa_attention",
    )(qkv, qkv, qkv, cos, sin, cos, sin, q_norm.reshape(1, hd), k_norm.reshape(1, hd))


def _s5_kernel(u_ref, perm_ref, m_ref, p_ref, q_ref, a_ref, y_ref, s_ref, h_ref, yg_ref, *, n_chunks, ctx_chunks, nb):
    w = m_ref.shape[1]
    lhs = u_ref[0]
    for g in range(m_ref.shape[0]):
        cols = slice(g * w, (g + 1) * w)
        u = _dot(lhs, perm_ref[:, cols]).astype(BF16)
        y = _s5_group(u, m_ref[g], p_ref[g], q_ref[g], a_ref[g], s_ref, h_ref, n_chunks, ctx_chunks, nb)
        yg_ref[:, cols] = y.astype(BF16)
    yg = yg_ref[...]
    for blk in range(m_ref.shape[0]):
        rows = slice(blk * w, (blk + 1) * w)
        y_ref[0, :, rows] = _dot_nt(yg, perm_ref[rows, :]).astype(y_ref.dtype)


def _s5_group(u, m, p, q, a, s_ref, h_ref, n_chunks, ctx_chunks, nb):
    half = S5_STATE
    s_ref[...] = _dot(u, p)
    a_re = a[0:1]
    a_im = a[1:2]
    is_fwd = lax.broadcasted_iota(jnp.int32, (nb, 2 * half), 1) < half

    def step(t, carry):
        h_re, h_im = carry
        k_rev = jnp.where(t < ctx_chunks, ctx_chunks - 1 - t, n_chunks - 1 - (t - ctx_chunks))
        rf = pl.multiple_of(t * nb, nb)
        rr = pl.multiple_of(k_rev * nb, nb)
        h_ref[pl.ds(rf, nb), 0:half] = h_re[:, 0:half]
        h_ref[pl.ds(rr, nb), half:2 * half] = h_re[:, half:2 * half]
        h_ref[pl.ds(rf, nb), 2 * half:3 * half] = h_im[:, 0:half]
        h_ref[pl.ds(rr, nb), 3 * half:4 * half] = h_im[:, half:2 * half]
        s_re = jnp.where(is_fwd, s_ref[pl.ds(rf, nb), 0:2 * half], s_ref[pl.ds(rr, nb), 0:2 * half])
        s_im = jnp.where(is_fwd, s_ref[pl.ds(rf, nb), 2 * half:4 * half], s_ref[pl.ds(rr, nb), 2 * half:4 * half])
        return a_re * h_re - a_im * h_im + s_re, a_re * h_im + a_im * h_re + s_im

    zero = jnp.zeros((nb, 2 * half), F32)
    lax.fori_loop(0, n_chunks, step, (zero, zero))
    return _dot(u, m) + _dot(h_ref[...].astype(BF16), q)


def _s5_tables(a_re, a_im, log_dt, b_re, b_im, c_re, c_im, d_skip):
    tc = S5_CHUNK
    n_groups = a_re.shape[1]
    lam = lax.complex(a_re.astype(F32), a_im.astype(F32))
    dt = jnp.exp(log_dt.astype(F32))[..., None]
    a_bar = jnp.exp(lam * dt)
    b_bar = ((a_bar - 1.0) / lam)[..., None] * lax.complex(b_re.astype(F32), b_im.astype(F32))
    cc = lax.complex(c_re.astype(F32), c_im.astype(F32))
    steps = jnp.arange(tc + 1, dtype=F32)
    apow = jnp.exp((lam * dt)[None] * steps[:, None, None, None])
    kern = jnp.einsum("dgcp,ldgp,dgpe->dlgce", cc, apow[:tc], b_bar).real
    tail = ((0, 0),) * 3
    kf = jnp.stack([jnp.pad(kern[0][:tc - j], ((j, 0),) + tail) for j in range(tc)])
    kr = jnp.stack([jnp.pad(kern[1][:j + 1][::-1], ((0, tc - 1 - j),) + tail) for j in range(tc)])
    m = jnp.transpose(kf + kr, (2, 0, 4, 1, 3))
    skip = d_skip.astype(F32).reshape(n_groups, S5_GROUP)
    eye_t = jnp.eye(tc, dtype=F32)
    eye_c = jnp.eye(S5_GROUP, dtype=F32)
    m = m + skip[:, None, :, None, None] * eye_t[None, :, None, :, None] * eye_c[None, None, :, None, :]
    m = m.reshape(n_groups, tc * S5_GROUP, tc * S5_GROUP)
    pf = jnp.einsum("jgp,gpe->gjep", apow[:tc][::-1, 0], b_bar[0])
    pr = jnp.einsum("jgp,gpe->gjep", apow[:tc, 1], b_bar[1])
    p = jnp.concatenate([pf.real, pr.real, pf.imag, pr.imag], axis=-1).reshape(n_groups, tc * S5_GROUP, 4 * S5_STATE)
    wf = jnp.einsum("gcp,igp->gpic", cc[0], apow[1:, 0])
    wr = jnp.einsum("gcp,igp->gpic", cc[1], apow[1:][::-1, 1])
    q = jnp.concatenate([wf.real, wr.real, -wf.imag, -wr.imag], axis=1).reshape(n_groups, 4 * S5_STATE, tc * S5_GROUP)
    a_tc = apow[tc]
    dec = jnp.stack([jnp.concatenate([a_tc[0].real, a_tc[1].real], axis=-1),
                     jnp.concatenate([a_tc[0].imag, a_tc[1].imag], axis=-1)], axis=1)
    dec = jnp.pad(dec, ((0, 0), (0, 6), (0, 0)))
    return m.astype(BF16), p.astype(BF16), q.astype(BF16), dec


def _s5_core(h, tables, ctx_len):
    b, s, d = h.shape
    tc = S5_CHUNK
    n_groups = d // S5_GROUP
    n_chunks = s // tc
    w = tc * S5_GROUP
    m, p, q, dec = tables
    slab = 128
    n_slabs = d // slab
    n_sub = slab // S5_GROUP
    rows = n_chunks * b
    u = h.reshape(b, n_chunks, tc, n_slabs, slab).transpose(3, 1, 0, 2, 4).reshape(n_slabs, rows, tc * slab)
    src = np.arange(tc * slab)
    dst = ((src % slab) // S5_GROUP) * w + (src // slab) * S5_GROUP + src % S5_GROUP
    perm = (jnp.asarray(dst)[:, None] == jnp.arange(tc * slab)[None, :]).astype(BF16)
    blk = lambda o: (o, 0, 0)
    y = pl.pallas_call(
        functools.partial(_s5_kernel, n_chunks=n_chunks, ctx_chunks=ctx_len // tc, nb=b),
        grid=(n_slabs,),
        in_specs=[
            pl.BlockSpec((1, rows, tc * slab), blk),
            _const_spec(perm.shape),
            pl.BlockSpec((n_sub, w, w), blk),
            pl.BlockSpec((n_sub, w, 4 * S5_STATE), blk),
            pl.BlockSpec((n_sub, 4 * S5_STATE, w), blk),
            pl.BlockSpec((n_sub, 8, 2 * S5_STATE), blk),
        ],
        out_specs=pl.BlockSpec((1, rows, tc * slab), blk),
        out_shape=jax.ShapeDtypeStruct((n_slabs, rows, tc * slab), BF16),
        scratch_shapes=[pltpu.VMEM((rows, 4 * S5_STATE), F32), pltpu.VMEM((rows, 4 * S5_STATE), F32),
                        pltpu.VMEM((rows, tc * slab), BF16)],
        compiler_params=_params("parallel"),
        name="s5_core",
    )(u, perm, m, p, q, dec)
    return y.reshape(n_slabs, n_chunks, b, tc, slab).transpose(2, 1, 3, 0, 4).reshape(b, s, d)


def _glu_res_kernel(y_ref, x_ref, md_ref, g_ref, w_ref, b_ref, o_ref):
    y = y_ref[0].astype(F32)
    gel = y * (0.5 * (1.0 + jnp.tanh(np.sqrt(2.0 / np.pi).astype(np.float32) * (y + 0.044715 * (y * y * y)))))
    out = gel * _sigmoid(_dot(gel.astype(BF16), w_ref[...]) + b_ref[...])
    o_ref[0] = x_ref[0] + md_ref[0, 0][2:3] * _rms(out, g_ref[1:2])


def _glu_res(y, xs, md, g, w, bias, ctx_tiles=1):
    b, s, d = xs.shape
    return pl.pallas_call(
        _glu_res_kernel,
        grid=(b, s // ROW_TILE),
        in_specs=[_row_spec(d), _row_spec(d), _mod_spec(d, ctx_tiles), _const_spec(g.shape),
                  _const_spec(w.shape), _const_spec((1, d))],
        out_specs=_row_spec(d),
        out_shape=jax.ShapeDtypeStruct((b, s, d), F32),
        compiler_params=_params("parallel", "parallel"),
        name="glu_res",
    )(y, xs, md, g, w, bias.reshape(1, d))


def _conv_silu_kernel(x_ref, prev_ref, next_ref, w_ref, b_ref, o_ref, *, ctx_len, seq_len, width):
    t = pl.program_id(1)
    tm = x_ref.shape[1]
    halo = prev_ref.shape[1]
    x = x_ref[0].astype(F32)
    lo = t * tm
    hi = lo + tm
    prev_ok = jnp.where((lo == 0) | (lo == ctx_len), 0.0, 1.0)
    next_ok = jnp.where((hi == ctx_len) | (hi == seq_len), 0.0, 1.0)
    prev = prev_ref[0].astype(F32) * prev_ok
    nxt = next_ref[0].astype(F32) * next_ok
    row = lax.broadcasted_iota(jnp.int32, (8, x.shape[1]), 0)
    prev_tail = prev[halo - 8:halo]
    next_head = nxt[0:8]
    half = width // 2
    acc = x * w_ref[half:half + 1] + b_ref[...]
    xb = x_ref[0]
    lag = lax.broadcasted_iota(jnp.int32, (tm, tm), 0) - lax.broadcasted_iota(jnp.int32, (tm, tm), 1)
    for k in range(1, half + 1):
        back = _dot(jnp.where(lag == k, 1.0, 0.0).astype(BF16), xb)
        top = jnp.where(row < k, pltpu.roll(prev_tail, k, 0), back[0:8])
        back = jnp.concatenate([top, back[8:]], axis=0)
        acc = acc + back * w_ref[half - k:half - k + 1]
        fwd = _dot(jnp.where(lag == -k, 1.0, 0.0).astype(BF16), xb)
        bottom = jnp.where(row >= 8 - k, pltpu.roll(next_head, 8 - k, 0), fwd[tm - 8:tm])
        fwd = jnp.concatenate([fwd[:tm - 8], bottom], axis=0)
        acc = acc + fwd * w_ref[half + k:half + k + 1]
    o_ref[0] = (acc * _sigmoid(acc)).astype(o_ref.dtype)


def _conv_silu(x, w, bias, ctx_len, width):
    assert x.dtype == BF16
    b, s, n = x.shape
    tm = ROW_TILE
    halo = 16
    sub = tm // halo
    last = s // halo - 1
    return pl.pallas_call(
        functools.partial(_conv_silu_kernel, ctx_len=ctx_len, seq_len=s, width=width),
        grid=(b, s // tm),
        in_specs=[
            _row_spec(n),
            pl.BlockSpec((1, halo, n), lambda i, t: (i, jnp.maximum(t * sub - 1, 0), 0)),
            pl.BlockSpec((1, halo, n), lambda i, t: (i, jnp.minimum((t + 1) * sub, last), 0)),
            _const_spec(w.shape),
            _const_spec((1, n)),
        ],
        out_specs=_row_spec(n),
        out_shape=jax.ShapeDtypeStruct((b, s, n), BF16),
        compiler_params=_params("parallel", "parallel"),
        name="conv_silu",
    )(x, x, x, w, bias.reshape(1, n))


def _softplus(z):
    return jnp.maximum(z, 0.0) + jnp.log1p(jnp.exp(-jnp.abs(z)))


def _expand_heads(v, e3):
    hi = v.astype(BF16)
    r1 = v - hi.astype(F32)
    mid = r1.astype(BF16)
    lo = (r1 - mid.astype(F32)).astype(BF16)
    return _dot(jnp.concatenate([hi, mid, lo], axis=1), e3)


def _ssd_kernel(*refs, rev, n_heads):
    if rev:
        (xs_ref, bm_ref, cm_ref, dtc_ref, dtr_ref, pr_ref, pc_ref, yprev_ref, dsk_ref, y_ref, st_ref) = refs
    else:
        (xs_ref, bm_ref, cm_ref, dtc_ref, dtr_ref, pr_ref, pc_ref, y_ref, st_ref) = refs
    t = pl.program_id(1)
    tc = xs_ref.shape[1]
    hd = SSD_HEAD_DIM
    pair = 2 * hd
    heads_per_group = n_heads // SSD_GROUPS

    @pl.when(t == 0)
    def _():
        st_ref[...] = jnp.zeros(st_ref.shape, F32)

    dtc = _softplus(dtc_ref[0, 0] + pr_ref[0:1])
    dtr = _softplus(dtr_ref[0, 0] + pc_ref[:, 0:1])
    dac = dtc * -jnp.exp(pr_ref[1:2])
    dar = dtr * -jnp.exp(pc_ref[:, 1:2])
    ii = lax.broadcasted_iota(jnp.int32, (tc, tc), 0)
    jj = lax.broadcasted_iota(jnp.int32, (tc, tc), 1)
    visible = (jj >= ii) if rev else (ii >= jj)
    tri = jnp.where(visible, 1.0, 0.0)
    hp = lax.Precision.HIGHEST
    cum_c = jnp.dot(tri, dac, precision=hp, preferred_element_type=F32)
    cum_r = lax.dot_general(dar, tri, (((1,), (1,)), ((), ())), precision=hp, preferred_element_type=F32)
    tot = cum_c[0:1] if rev else cum_c[tc - 1:tc]

    e_row = lax.broadcasted_iota(jnp.int32, (3 * n_heads, n_heads * hd), 0)
    e_col = lax.broadcasted_iota(jnp.int32, (3 * n_heads, n_heads * hd), 1)
    e3 = jnp.where((e_row % n_heads) == (e_col // hd), 1.0, 0.0).astype(BF16)
    w_out = _expand_heads(dtc * jnp.exp(tot - cum_c), e3)
    w_in = _expand_heads(jnp.exp(cum_c), e3)
    dec = _expand_heads(jnp.broadcast_to(jnp.exp(tot), (8, n_heads)), e3)[0:1]

    xs = xs_ref[0]
    lane = lax.broadcasted_iota(jnp.int32, (tc, pair), 1)
    first = lane < hd
    for g in range(SSD_GROUPS):
        bmb = bm_ref[0, :, g * SSD_STATE:(g + 1) * SSD_STATE].astype(BF16)
        cmb = cm_ref[0, :, g * SSD_STATE:(g + 1) * SSD_STATE].astype(BF16)
        scores = _dot_nt(cmb, bmb)
        for k in range(heads_per_group // 2):
            hp_i = g * (heads_per_group // 2) + k
            cols = slice(hp_i * pair, (hp_i + 1) * pair)
            x2b = xs[:, cols]
            x2 = x2b.astype(F32)
            halves = []
            for s_i in range(2):
                h = 2 * hp_i + s_i
                decay = jnp.exp(jnp.where(visible, cum_c[:, h:h + 1] - cum_r[h:h + 1, :], -jnp.inf))
                wgt = (scores * decay * dtr[h:h + 1, :]).astype(BF16)
                halves.append(_dot(wgt, x2b))
            y_diag = jnp.where(first, halves[0], halves[1])
            st = st_ref[hp_i]
            y_off = _dot(cmb, st.astype(BF16)) * w_in[:, cols]
            y = y_diag + y_off
            if rev:
                y = y + yprev_ref[0, :, cols].astype(F32) + x2 * dsk_ref[:, cols]
            y_ref[0, :, cols] = y.astype(y_ref.dtype)
            st_ref[hp_i] = dec[:, cols] * st + _dot_tn(bmb, (x2 * w_out[:, cols]).astype(BF16))


def _ssd_scan(xbc, dt_cols, dt_rows, prm_rows, prm_cols, d_inner, ctx_len, rev, y_prev=None, d_skip=None):
    b, s, _ = xbc.shape
    tc = SSD_CHUNK
    n_heads = d_inner // SSD_HEAD_DIM
    nbc = SSD_GROUPS * SSD_STATE
    n_chunks = s // tc
    ctx_chunks = ctx_len // tc
    d = 1 if rev else 0
    if rev:
        chunk = lambda t: jnp.where(t < ctx_chunks, ctx_chunks - 1 - t, n_chunks - 1 - (t - ctx_chunks))
    else:
        chunk = lambda t: t
    in_specs = [
        pl.BlockSpec((1, tc, d_inner), lambda i, t: (i, chunk(t), 0)),
        pl.BlockSpec((1, tc, nbc), lambda i, t: (i, chunk(t), d_inner // nbc)),
        pl.BlockSpec((1, tc, nbc), lambda i, t: (i, chunk(t), d_inner // nbc + 1)),
        pl.BlockSpec((1, 1, tc, n_heads), lambda i, t: (i, d, chunk(t), 0)),
        pl.BlockSpec((1, 1, n_heads, tc), lambda i, t: (i, d, 0, chunk(t))),
        _const_spec(prm_rows.shape),
        _const_spec(prm_cols.shape),
    ]
    args = [xbc, xbc, xbc, dt_cols, dt_rows, prm_rows, prm_cols]
    if rev:
        in_specs += [pl.BlockSpec((1, tc, d_inner), lambda i, t: (i, chunk(t), 0)), _const_spec((1, d_inner))]
        args += [y_prev, d_skip]
    return pl.pallas_call(
        functools.partial(_ssd_kernel, rev=rev, n_heads=n_heads),
        grid=(b, n_chunks),
        in_specs=in_specs,
        out_specs=pl.BlockSpec((1, tc, d_inner), lambda i, t: (i, chunk(t), 0)),
        out_shape=jax.ShapeDtypeStruct((b, s, d_inner), BF16),
        scratch_shapes=[pltpu.VMEM((n_heads // 2, SSD_STATE, 2 * SSD_HEAD_DIM), F32)],
        compiler_params=_params("parallel", "arbitrary"),
        name="ssd_rev" if rev else "ssd_fwd",
    )(*args)


def _ssd_out_kernel(y_ref, gt_ref, x_ref, md_ref, g_ref, ng_ref, w_ref, o_ref):
    gt = gt_ref[0].astype(F32)
    yg = y_ref[0].astype(F32) * (gt * _sigmoid(gt))
    width = yg.shape[1] // SSD_GROUPS
    r = jnp.zeros(x_ref.shape[1:], F32)
    for k in range(SSD_GROUPS):
        seg = _rms(yg[:, k * width:(k + 1) * width], ng_ref[:, k * width:(k + 1) * width])
        r = r + _dot(seg.astype(BF16), w_ref[k * width:(k + 1) * width, :])
    o_ref[0] = x_ref[0] + md_ref[0, 0][2:3] * _rms(r, g_ref[1:2])


def _ssd_out(y, gate, xs, md, g, norm_g, w, ctx_tiles=1):
    b, s, d = xs.shape
    n = y.shape[-1]
    return pl.pallas_call(
        _ssd_out_kernel,
        grid=(b, s // ROW_TILE),
        in_specs=[_row_spec(n), _row_spec(n), _row_spec(d), _mod_spec(d, ctx_tiles), _const_spec(g.shape),
                  _const_spec((1, n)), _const_spec(w.shape)],
        out_specs=_row_spec(d),
        out_shape=jax.ShapeDtypeStruct((b, s, d), F32),
        compiler_params=_params("parallel", "parallel"),
        name="ssd_out",
    )(y, gate, xs, md, g, norm_g.reshape(1, n), w)


def _na_kernel(q_ref, k_ref, v_ref, bias_ref, o_ref, *, ctx_len, grid_rows):
    r = pl.program_id(1)
    hd = NA_HEAD_DIM
    pair = 2 * hd
    n_heads = q_ref.shape[2] // hd
    win = NA_WIN_ROWS * GRID_W
    r0 = jnp.clip(r - NA_WIN_ROWS // 2, 0, grid_rows - NA_WIN_ROWS)
    base = pl.multiple_of(ctx_len + r0 * GRID_W, GRID_W)
    lane = lax.broadcasted_iota(jnp.int32, (GRID_W, pair), 1)
    first = lane < hd
    scale = hd ** -0.5
    s_loc, s_ctx = [], []
    for h in range(n_heads):
        cols = slice((h // 2) * pair, (h // 2 + 1) * pair)
        qm = jnp.where(first if h % 2 == 0 else ~first, q_ref[0, :, cols] * scale, 0.0).astype(BF16)
        s_loc.append(_dot_nt(qm, k_ref[0, pl.ds(base, win), cols]) + bias_ref[0, h])
        s_ctx.append(_dot_nt(qm, k_ref[0, 0:ctx_len, cols]))
    p_loc, p_ctx, inv = [], [], []
    for h in range(n_heads):
        m = jnp.maximum(jnp.max(s_loc[h], axis=-1, keepdims=True), jnp.max(s_ctx[h], axis=-1, keepdims=True))
        e_loc = jnp.exp(s_loc[h] - m)
        e_ctx = jnp.exp(s_ctx[h] - m)
        inv.append(1.0 / (jnp.sum(e_loc, axis=-1, keepdims=True) + jnp.sum(e_ctx, axis=-1, keepdims=True)))
        p_loc.append(e_loc.astype(BF16))
        p_ctx.append(e_ctx.astype(BF16))
    for hp_i in range(n_heads // 2):
        cols = slice(hp_i * pair, (hp_i + 1) * pair)
        vw = v_ref[0, pl.ds(base, win), cols]
        vc = v_ref[0, 0:ctx_len, cols]
        halves = [(_dot(p_loc[h], vw) + _dot(p_ctx[h], vc)) * inv[h] for h in (2 * hp_i, 2 * hp_i + 1)]
        o_ref[0, :, cols] = jnp.where(first, halves[0], halves[1]).astype(o_ref.dtype)


def _na_bias_table(rpb):
    n_heads, n_rr, n_cr = rpb.shape
    qc = np.arange(GRID_W)[:, None]
    kc = np.arange(GRID_W)[None, :]
    start = np.clip(qc - NA_WIN_COLS // 2, 0, GRID_W - NA_WIN_COLS)
    inside = (kc >= start) & (kc < start + NA_WIN_COLS)
    col_rel = kc - qc + (NA_WIN_COLS - 1)
    onehot = ((col_rel[None] == np.arange(n_cr)[:, None, None]) & inside[None]).astype(np.float32)
    t2 = jnp.dot(rpb.astype(F32).reshape(n_heads * n_rr, n_cr), jnp.asarray(onehot.reshape(n_cr, GRID_W * GRID_W)),
                 precision=lax.Precision.HIGHEST).reshape(n_heads, n_rr, GRID_W, GRID_W)
    t2 = jnp.where(jnp.asarray(inside), t2, NEG_BIG)
    tbl = jnp.stack([t2[:, NA_WIN_ROWS - 1 - dr:2 * NA_WIN_ROWS - 1 - dr] for dr in range(NA_WIN_ROWS)])
    return jnp.transpose(tbl, (0, 1, 3, 2, 4)).reshape(NA_WIN_ROWS, n_heads, GRID_W, NA_WIN_ROWS * GRID_W)


def _na_attention(q, k, v, bias, ctx_len):
    b, s, d = q.shape
    n_lat = s - ctx_len
    grid_rows = n_lat // GRID_W
    n_heads = d // NA_HEAD_DIM
    q_off = ctx_len // GRID_W

    def bias_index(i, r):
        return (r - jnp.clip(r - NA_WIN_ROWS // 2, 0, grid_rows - NA_WIN_ROWS), 0, 0, 0)

    return pl.pallas_call(
        functools.partial(_na_kernel, ctx_len=ctx_len, grid_rows=grid_rows),
        grid=(b, grid_rows),
        in_specs=[
            pl.BlockSpec((1, GRID_W, d), lambda i, r: (i, q_off + r, 0)),
            pl.BlockSpec((1, s, d), lambda i, r: (i, 0, 0)),
            pl.BlockSpec((1, s, d), lambda i, r: (i, 0, 0)),
            pl.BlockSpec((1, n_heads, GRID_W, NA_WIN_ROWS * GRID_W), bias_index),
        ],
        out_specs=pl.BlockSpec((1, GRID_W, d), lambda i, r: (i, r, 0)),
        out_shape=jax.ShapeDtypeStruct((b, n_lat, d), BF16),
        compiler_params=_params("parallel", "arbitrary"),
        name="na_attention",
    )(q, k, v, bias)


def _forward(x, c, ctx, c_ctx, ada_w, ada_b, norm_g, mlp_w1, mlp_w2,
             gqa_w_qkv, gqa_q_norm, gqa_k_norm, gqa_w_o,
             s5_a_re, s5_a_im, s5_log_dt, s5_b_re, s5_b_im, s5_c_re, s5_c_im, s5_d, s5_glu_w, s5_glu_b,
             ssd_w_in, ssd_conv_w, ssd_conv_b, ssd_dt_bias, ssd_a_log, ssd_d, ssd_norm_g, ssd_w_out,
             na_w_qkv, na_rpb, na_w_o):
    bsz, n_lat, d = x.shape
    ctx_len = ctx.shape[1]
    depth = ada_w.shape[0]
    n_mod = ada_w.shape[2] // d
    assert depth == 4 and ctx_len == ROW_TILE and n_lat % ROW_TILE == 0 and n_lat % GRID_W == 0
    assert n_lat // GRID_W >= NA_WIN_ROWS

    cond = jnp.concatenate([c, c_ctx[None]], axis=0)
    cond = jnp.pad(cond, ((0, -cond.shape[0] % 8), (0, 0)))
    ada = _ada(cond, ada_w, ada_b)

    xs = jnp.concatenate([ctx, x], axis=1)
    streams = []
    for i in range(depth):
        kind = i % 4
        j = i // 4
        mod = ada[i, :bsz].reshape(bsz, n_mod, d)
        modc = jnp.broadcast_to(ada[i, bsz].reshape(1, n_mod, d), (bsz, n_mod, d))
        md = jnp.pad(jnp.stack([modc, mod], axis=1), ((0, 0), (0, 0), (0, 8 - n_mod), (0, 0)))
        g = jnp.pad(norm_g[i], ((0, 4), (0, 0)))
        w1 = mlp_w1[i].astype(BF16)
        w2 = mlp_w2[i].astype(BF16)
        ctx_tiles = 1
        if kind == 0:
            w_qkv = gqa_w_qkv[j].astype(BF16)
            n = w_qkv.shape[1]
            (qkv,) = _nm_matmul(xs, md, g, w_qkv, ((0, n),), (F32,))
            o = _gqa_attention(qkv, gqa_q_norm[j], gqa_k_norm[j], ctx_len)
            xs = _proj_res(o, xs, md, g, gqa_w_o[j].astype(BF16))
        elif kind == 1:
            h = _nm(xs, md, g, BF16)
            tables = _s5_tables(s5_a_re[j], s5_a_im[j], s5_log_dt[j], s5_b_re[j], s5_b_im[j],
                                s5_c_re[j], s5_c_im[j], s5_d[j])
            y = _s5_core(h, tables, ctx_len)
            xs = _glu_res(y, xs, md, g, s5_glu_w[j].astype(BF16), s5_glu_b[j])
        elif kind == 2:
            w_in = ssd_w_in[j].astype(BF16)
            d_inner = ssd_w_out.shape[1]
            n_heads = d_inner // SSD_HEAD_DIM
            conv_ch = ssd_conv_w.shape[2]
            splits = ((0, d_inner), (d_inner, d_inner + conv_ch), (d_inner + conv_ch, w_in.shape[1]))
            gate, pre, dt_raw = _nm_matmul(xs, md, g, w_in, splits, (BF16, BF16, F32))
            conv_w = jnp.pad(ssd_conv_w[j], ((0, 8 - ssd_conv_w.shape[1]), (0, 0)))
            xbc = _conv_silu(pre, conv_w, ssd_conv_b[j], ctx_len, ssd_conv_w.shape[1])
            dt_cols = dt_raw.reshape(bsz, -1, 2, n_heads).transpose(0, 2, 1, 3)
            dt_rows = dt_cols.transpose(0, 1, 3, 2)
            d_skip = jnp.repeat(ssd_d[j].astype(F32), SSD_HEAD_DIM).reshape(1, d_inner)
            y = None
            for direction in (0, 1):
                prm = jnp.stack([ssd_dt_bias[j, direction], ssd_a_log[j, direction]]).astype(F32)
                prm_rows = jnp.pad(prm, ((0, 6), (0, 0)))
                prm_cols = jnp.pad(prm.T, ((0, 0), (0, 6)))
                y = _ssd_scan(xbc, dt_cols, dt_rows, prm_rows, prm_cols, d_inner, ctx_len, direction == 1,
                              y_prev=y, d_skip=d_skip)
            xs = _ssd_out(y, gate, xs, md, g, ssd_norm_g[j], ssd_w_out[j].astype(BF16))
        else:
            w_qkv = na_w_qkv[j].astype(BF16)
            q, k, v = _nm_matmul(xs, md, g, w_qkv, ((0, d), (d, 2 * d), (2 * d, 3 * d)), (F32, BF16, BF16))
            o = _na_attention(q, k, v, _na_bias_table(na_rpb[j]), ctx_len)
            ctx_tiles = 0
            xs = _proj_res(o, xs, md, g, na_w_o[j].astype(BF16), ctx_tiles=ctx_tiles,
                           skip_tiles=ctx_len // ROW_TILE)
        xs = _mlp(xs, md, g, w1, w2, ctx_tiles=ctx_tiles)
        streams.append(xs)
    return streams


def kernel(x, c, ctx, c_ctx, ada_w, ada_b, norm_g, mlp_w1, mlp_w2, gqa_w_qkv, gqa_q_norm, gqa_k_norm, gqa_w_o, s5_a_re, s5_a_im, s5_log_dt, s5_b_re, s5_b_im, s5_c_re, s5_c_im, s5_d, s5_glu_w, s5_glu_b, ssd_w_in, ssd_conv_w, ssd_conv_b, ssd_dt_bias, ssd_a_log, ssd_d, ssd_norm_g, ssd_w_out, na_w_qkv, na_rpb, na_w_o):
    return _forward(x, c, ctx, c_ctx, ada_w, ada_b, norm_g, mlp_w1, mlp_w2, gqa_w_qkv, gqa_q_norm, gqa_k_norm,
                    gqa_w_o, s5_a_re, s5_a_im, s5_log_dt, s5_b_re, s5_b_im, s5_c_re, s5_c_im, s5_d, s5_glu_w,
                    s5_glu_b, ssd_w_in, ssd_conv_w, ssd_conv_b, ssd_dt_bias, ssd_a_log, ssd_d, ssd_norm_g,
                    ssd_w_out, na_w_qkv, na_rpb, na_w_o)[-1]
```

```python
import functools

import numpy as np
import jax
import jax.numpy as jnp
from jax import lax
from jax.experimental import pallas as pl
from jax.experimental.pallas import tpu as pltpu

F32 = jnp.float32
BF16 = jnp.bfloat16

NORM_EPS = 1e-6
GRID_W = 64
ROW_TILE = 256
VMEM_LIMIT_BYTES = 56 * 1024 * 1024
NEG_BIG = -1e30

GQA_HEAD_DIM = 128
GQA_KV_HEADS = 2
ROPE_THETA = 10000.0
S5_GROUP = 16
S5_STATE = 64
S5_CHUNK = 16
SSD_HEAD_DIM = 64
SSD_GROUPS = 4
SSD_STATE = 128
SSD_CHUNK = 128
NA_HEAD_DIM = 64
NA_WIN_ROWS = 8
NA_WIN_COLS = 16


def _params(*semantics):
    return pltpu.CompilerParams(dimension_semantics=semantics, vmem_limit_bytes=VMEM_LIMIT_BYTES)


def _rms(y, g):
    return y * lax.rsqrt(jnp.mean(y * y, axis=-1, keepdims=True) + NORM_EPS) * g


def _dot(a, b):
    return jnp.dot(a, b, preferred_element_type=F32)


def _dot_nt(a, b):
    return lax.dot_general(a, b, (((1,), (1,)), ((), ())), preferred_element_type=F32)


def _dot_tn(a, b):
    return lax.dot_general(a, b, (((0,), (0,)), ((), ())), preferred_element_type=F32)


def _sigmoid(z):
    return 1.0 / (1.0 + jnp.exp(-z))


def _const_spec(shape):
    nd = len(shape)
    return pl.BlockSpec(shape, lambda *_: (0,) * nd)


def _mod_spec(d, ctx_tiles):
    return pl.BlockSpec((1, 1, 8, d), lambda b, t: (b, jnp.where(t < ctx_tiles, 0, 1), 0, 0))


def _row_spec(n, tm=ROW_TILE):
    return pl.BlockSpec((1, tm, n), lambda b, t: (b, t, 0))


def _ada_kernel(c_ref, w_ref, b_ref, o_ref):
    cc = c_ref[...]
    s = cc * _sigmoid(cc)
    o_ref[0] = _dot(s.astype(BF16), w_ref[0].astype(BF16)) + b_ref[0]


def _ada(cond, ada_w, ada_b):
    depth, d, n = ada_w.shape
    rows = cond.shape[0]
    tn = 1536
    return pl.pallas_call(
        _ada_kernel,
        grid=(depth, n // tn),
        in_specs=[
            pl.BlockSpec((rows, d), lambda i, j: (0, 0)),
            pl.BlockSpec((1, d, tn), lambda i, j: (i, 0, j)),
            pl.BlockSpec((1, 1, tn), lambda i, j: (i, 0, j)),
        ],
        out_specs=pl.BlockSpec((1, rows, tn), lambda i, j: (i, 0, j)),
        out_shape=jax.ShapeDtypeStruct((depth, rows, n), F32),
        compiler_params=_params("parallel", "parallel"),
        name="ada",
    )(cond, ada_w, ada_b.reshape(depth, 1, n))


def _nm_matmul_kernel(x_ref, md_ref, g_ref, w_ref, *o_refs, splits):
    md = md_ref[0, 0]
    h = _rms(x_ref[0], g_ref[0:1]) * (1.0 + md[1:2]) + md[0:1]
    hb = h.astype(BF16)
    for o_ref, (lo, hi) in zip(o_refs, splits):
        o_ref[0] = _dot(hb, w_ref[:, lo:hi]).astype(o_ref.dtype)


def _nm_matmul(xs, md, g, w, splits, dtypes, ctx_tiles=1):
    b, s, d = xs.shape
    out_shape = [jax.ShapeDtypeStruct((b, s, hi - lo), dt) for (lo, hi), dt in zip(splits, dtypes)]
    return pl.pallas_call(
        functools.partial(_nm_matmul_kernel, splits=splits),
        grid=(b, s // ROW_TILE),
        in_specs=[_row_spec(d), _mod_spec(d, ctx_tiles), _const_spec(g.shape), _const_spec(w.shape)],
        out_specs=[_row_spec(hi - lo) for lo, hi in splits],
        out_shape=out_shape,
        compiler_params=_params("parallel", "parallel"),
        name="nm_matmul",
    )(xs, md, g, w)


def _nm_kernel(x_ref, md_ref, g_ref, o_ref):
    md = md_ref[0, 0]
    h = _rms(x_ref[0], g_ref[0:1]) * (1.0 + md[1:2]) + md[0:1]
    o_ref[0] = h.astype(o_ref.dtype)


def _nm(xs, md, g, dtype, ctx_tiles=1):
    b, s, d = xs.shape
    return pl.pallas_call(
        _nm_kernel,
        grid=(b, s // ROW_TILE),
        in_specs=[_row_spec(d), _mod_spec(d, ctx_tiles), _const_spec(g.shape)],
        out_specs=_row_spec(d),
        out_shape=jax.ShapeDtypeStruct((b, s, d), dtype),
        compiler_params=_params("parallel", "parallel"),
        name="nm",
    )(xs, md, g)


def _mlp_kernel(x_ref, md_ref, g_ref, w1_ref, w2_ref, o_ref, *, ff_chunk):
    x = x_ref[0]
    md = md_ref[0, 0]
    hb = (_rms(x, g_ref[2:3]) * (1.0 + md[4:5]) + md[3:4]).astype(BF16)
    acc = jnp.zeros(x.shape, F32)
    for lo in range(0, w1_ref.shape[1], ff_chunk):
        a = _dot(hb, w1_ref[:, lo:lo + ff_chunk])
        a = jnp.square(jnp.maximum(a, 0.0)).astype(BF16)
        acc = acc + _dot(a, w2_ref[lo:lo + ff_chunk, :])
    o_ref[0] = x + md[5:6] * _rms(acc, g_ref[3:4])


def _mlp(xs, md, g, w1, w2, ctx_tiles=1):
    b, s, d = xs.shape
    return pl.pallas_call(
        functools.partial(_mlp_kernel, ff_chunk=1024),
        grid=(b, s // ROW_TILE),
        in_specs=[_row_spec(d), _mod_spec(d, ctx_tiles), _const_spec(g.shape),
                  _const_spec(w1.shape), _const_spec(w2.shape)],
        out_specs=_row_spec(d),
        out_shape=jax.ShapeDtypeStruct((b, s, d), F32),
        compiler_params=_params("parallel", "parallel"),
        name="mlp",
    )(xs, md, g, w1, w2)


def _proj_res_kernel(y_ref, x_ref, md_ref, g_ref, w_ref, o_ref):
    r = _dot(y_ref[0].astype(BF16), w_ref[...])
    o_ref[0] = x_ref[0] + md_ref[0, 0][2:3] * _rms(r, g_ref[1:2])


def _proj_res(y, xs, md, g, w, ctx_tiles=1, skip_tiles=0):
    b, _, d = xs.shape
    s = y.shape[1]
    return pl.pallas_call(
        _proj_res_kernel,
        grid=(b, s // ROW_TILE),
        in_specs=[_row_spec(y.shape[-1]), pl.BlockSpec((1, ROW_TILE, d), lambda i, t: (i, t + skip_tiles, 0)),
                  _mod_spec(d, ctx_tiles), _const_spec(g.shape), _const_spec(w.shape)],
        out_specs=_row_spec(d),
        out_shape=jax.ShapeDtypeStruct((b, s, d), F32),
        compiler_params=_params("parallel", "parallel"),
        name="proj_res",
    )(y, xs, md, g, w)


def _rope(x, cos, sin_signed):
    lane = lax.broadcasted_iota(jnp.int32, x.shape, 1)
    quarter = GQA_HEAD_DIM // 4
    partner = jnp.where((lane % (2 * quarter)) < quarter,
                        pltpu.roll(x, GQA_HEAD_DIM - quarter, 1), pltpu.roll(x, quarter, 1))
    return x * cos + partner * sin_signed


SOFTMAX_ROW_BLOCK = 16


def _softmax_rows(s):
    ps, sums = [], []
    for lo in range(0, s.shape[0], SOFTMAX_ROW_BLOCK):
        sb = s[lo:lo + SOFTMAX_ROW_BLOCK]
        e = jnp.exp(sb - jnp.max(sb, axis=-1, keepdims=True))
        sums.append(jnp.sum(e, axis=-1, keepdims=True))
        ps.append(e.astype(BF16))
    return jnp.concatenate(ps, axis=0), jnp.concatenate(sums, axis=0)


def _gqa_kernel(q_ref, k_ref, v_ref, cq_ref, sq_ref, ck_ref, sk_ref, qn_ref, kn_ref, o_ref, kb_ref, vb_ref,
                *, ctx_len, group):
    t = pl.program_id(2)
    tq = q_ref.shape[1]
    s_all = k_ref.shape[1]
    hd = GQA_HEAD_DIM

    @pl.when(t == 0)
    def _():
        kn = _rms(k_ref[0], kn_ref[...])
        kb_ref[...] = _rope(kn, ck_ref[...], sk_ref[...]).astype(BF16)
        vb_ref[...] = v_ref[0].astype(BF16)

    cos = cq_ref[...]
    sin = sq_ref[...]
    scale = hd ** -0.5

    def attend(n_keys):
        kb = kb_ref[0:n_keys]
        vb = vb_ref[0:n_keys]
        for g in range(group):
            q = q_ref[0, :, g * hd:(g + 1) * hd]
            qb = (_rope(_rms(q, qn_ref[...]), cos, sin) * scale).astype(BF16)
            p, denom = _softmax_rows(_dot_nt(qb, kb))
            o_ref[0, :, g * hd:(g + 1) * hd] = (_dot(p, vb) / denom).astype(o_ref.dtype)

    @pl.when(t * tq < ctx_len)
    def _():
        attend(ctx_len)

    @pl.when(t * tq >= ctx_len)
    def _():
        attend(s_all)


def _rope_tables(n_lat, ctx_len):
    pos = np.arange(n_lat)
    row = (pos // GRID_W).astype(np.float32)
    col = (pos % GRID_W).astype(np.float32)
    half = GQA_HEAD_DIM // 2
    inv_freq = (1.0 / (ROPE_THETA ** (np.arange(0, half, 2, dtype=np.float32) / half))).astype(np.float32)
    ar = row[:, None] * inv_freq
    ac = col[:, None] * inv_freq
    cos = np.concatenate([np.cos(ar), np.cos(ar), np.cos(ac), np.cos(ac)], axis=1)
    sin = np.concatenate([-np.sin(ar), np.sin(ar), -np.sin(ac), np.sin(ac)], axis=1)
    cos = np.concatenate([np.ones((ctx_len, GQA_HEAD_DIM)), cos], axis=0).astype(np.float32)
    sin = np.concatenate([np.zeros((ctx_len, GQA_HEAD_DIM)), sin], axis=0).astype(np.float32)
    return jnp.asarray(cos), jnp.asarray(sin)


def _gqa_attention(qkv, q_norm, k_norm, ctx_len):
    b, s, n = qkv.shape
    hd = GQA_HEAD_DIM
    n_q = n - 2 * GQA_KV_HEADS * hd
    group = n_q // hd // GQA_KV_HEADS
    cos, sin = _rope_tables(s - ctx_len, ctx_len)
    tq = ROW_TILE
    q_blocks = n_q // hd
    return pl.pallas_call(
        functools.partial(_gqa_kernel, ctx_len=ctx_len, group=group),
        grid=(b, GQA_KV_HEADS, s // tq),
        in_specs=[
            pl.BlockSpec((1, tq, group * hd), lambda i, h, t: (i, t, h)),
            pl.BlockSpec((1, s, hd), lambda i, h, t: (i, 0, q_blocks + h)),
            pl.BlockSpec((1, s, hd), lambda i, h, t: (i, 0, q_blocks + GQA_KV_HEADS + h)),
            pl.BlockSpec((tq, hd), lambda i, h, t: (t, 0)),
            pl.BlockSpec((tq, hd), lambda i, h, t: (t, 0)),
            _const_spec((s, hd)),
            _const_spec((s, hd)),
            _const_spec((1, hd)),
            _const_spec((1, hd)),
        ],
        out_specs=pl.BlockSpec((1, tq, group * hd), lambda i, h, t: (i, t, h)),
        out_shape=jax.ShapeDtypeStruct((b, s, n_q), BF16),
        scratch_shapes=[pltpu.VMEM((s, hd), BF16), pltpu.VMEM((s, hd), BF16)],
        compiler_params=_params("parallel", "parallel", "arbitrary"),
        name="gqa_attention",
    )(qkv, qkv, qkv, cos, sin, cos, sin, q_norm.reshape(1, hd), k_norm.reshape(1, hd))


def _s5_kernel(u_ref, perm_ref, m_ref, p_ref, q_ref, a_ref, y_ref, s_ref, h_ref, yg_ref, *, n_chunks, ctx_chunks, nb):
    w = m_ref.shape[1]
    lhs = u_ref[0]
    for g in range(m_ref.shape[0]):
        cols = slice(g * w, (g + 1) * w)
        u = _dot(lhs, perm_ref[:, cols]).astype(BF16)
        y = _s5_group(u, m_ref[g], p_ref[g], q_ref[g], a_ref[g], s_ref, h_ref, n_chunks, ctx_chunks, nb)
        yg_ref[:, cols] = y.astype(BF16)
    yg = yg_ref[...]
    for blk in range(m_ref.shape[0]):
        rows = slice(blk * w, (blk + 1) * w)
        y_ref[0, :, rows] = _dot_nt(yg, perm_ref[rows, :]).astype(y_ref.dtype)


def _s5_group(u, m, p, q, a, s_ref, h_ref, n_chunks, ctx_chunks, nb):
    half = S5_STATE
    s_ref[...] = _dot(u, p)
    a_re = a[0:1]
    a_im = a[1:2]
    is_fwd = lax.broadcasted_iota(jnp.int32, (nb, 2 * half), 1) < half

    def step(t, carry):
        h_re, h_im = carry
        k_rev = jnp.where(t < ctx_chunks, ctx_chunks - 1 - t, n_chunks - 1 - (t - ctx_chunks))
        rf = pl.multiple_of(t * nb, nb)
        rr = pl.multiple_of(k_rev * nb, nb)
        h_ref[pl.ds(rf, nb), 0:half] = h_re[:, 0:half]
        h_ref[pl.ds(rr, nb), half:2 * half] = h_re[:, half:2 * half]
        h_ref[pl.ds(rf, nb), 2 * half:3 * half] = h_im[:, 0:half]
        h_ref[pl.ds(rr, nb), 3 * half:4 * half] = h_im[:, half:2 * half]
        s_re = jnp.where(is_fwd, s_ref[pl.ds(rf, nb), 0:2 * half], s_ref[pl.ds(rr, nb), 0:2 * half])
        s_im = jnp.where(is_fwd, s_ref[pl.ds(rf, nb), 2 * half:4 * half], s_ref[pl.ds(rr, nb), 2 * half:4 * half])
        return a_re * h_re - a_im * h_im + s_re, a_re * h_im + a_im * h_re + s_im

    zero = jnp.zeros((nb, 2 * half), F32)
    lax.fori_loop(0, n_chunks, step, (zero, zero))
    return _dot(u, m) + _dot(h_ref[...].astype(BF16), q)


def _s5_tables(a_re, a_im, log_dt, b_re, b_im, c_re, c_im, d_skip):
    tc = S5_CHUNK
    n_groups = a_re.shape[1]
    lam = lax.complex(a_re.astype(F32), a_im.astype(F32))
    dt = jnp.exp(log_dt.astype(F32))[..., None]
    a_bar = jnp.exp(lam * dt)
    b_bar = ((a_bar - 1.0) / lam)[..., None] * lax.complex(b_re.astype(F32), b_im.astype(F32))
    cc = lax.complex(c_re.astype(F32), c_im.astype(F32))
    steps = jnp.arange(tc + 1, dtype=F32)
    apow = jnp.exp((lam * dt)[None] * steps[:, None, None, None])
    kern = jnp.einsum("dgcp,ldgp,dgpe->dlgce", cc, apow[:tc], b_bar).real
    jj = np.arange(tc)[:, None, None]
    ii = np.arange(tc)[None, :, None]
    ll = np.arange(tc)[None, None, :]
    pick = jnp.asarray(np.stack([ii - jj == ll, jj - ii == ll]).astype(np.float32))
    toeplitz = jnp.einsum("djil,dlgce->jigce", pick, kern, precision=lax.Precision.HIGHEST)
    m = jnp.transpose(toeplitz, (2, 0, 4, 1, 3))
    skip = d_skip.astype(F32).reshape(n_groups, S5_GROUP)
    eye_t = jnp.eye(tc, dtype=F32)
    eye_c = jnp.eye(S5_GROUP, dtype=F32)
    m = m + skip[:, None, :, None, None] * eye_t[None, :, None, :, None] * eye_c[None, None, :, None, :]
    m = m.reshape(n_groups, tc * S5_GROUP, tc * S5_GROUP)
    pf = jnp.einsum("jgp,gpe->gjep", apow[:tc][::-1, 0], b_bar[0])
    pr = jnp.einsum("jgp,gpe->gjep", apow[:tc, 1], b_bar[1])
    p = jnp.concatenate([pf.real, pr.real, pf.imag, pr.imag], axis=-1).reshape(n_groups, tc * S5_GROUP, 4 * S5_STATE)
    wf = jnp.einsum("gcp,igp->gpic", cc[0], apow[1:, 0])
    wr = jnp.einsum("gcp,igp->gpic", cc[1], apow[1:][::-1, 1])
    q = jnp.concatenate([wf.real, wr.real, -wf.imag, -wr.imag], axis=1).reshape(n_groups, 4 * S5_STATE, tc * S5_GROUP)
    a_tc = apow[tc]
    dec = jnp.stack([jnp.concatenate([a_tc[0].real, a_tc[1].real], axis=-1),
                     jnp.concatenate([a_tc[0].imag, a_tc[1].imag], axis=-1)], axis=1)
    dec = jnp.pad(dec, ((0, 0), (0, 6), (0, 0)))
    return m.astype(BF16), p.astype(BF16), q.astype(BF16), dec


def _s5_core(h, tables, ctx_len):
    b, s, d = h.shape
    tc = S5_CHUNK
    n_groups = d // S5_GROUP
    n_chunks = s // tc
    w = tc * S5_GROUP
    m, p, q, dec = tables
    slab = 128
    n_slabs = d // slab
    n_sub = slab // S5_GROUP
    rows = n_chunks * b
    u = h.reshape(b, n_chunks, tc, n_slabs, slab).transpose(3, 1, 0, 2, 4).reshape(n_slabs, rows, tc * slab)
    src = np.arange(tc * slab)
    dst = ((src % slab) // S5_GROUP) * w + (src // slab) * S5_GROUP + src % S5_GROUP
    perm = (jnp.asarray(dst)[:, None] == jnp.arange(tc * slab)[None, :]).astype(BF16)
    blk = lambda o: (o, 0, 0)
    y = pl.pallas_call(
        functools.partial(_s5_kernel, n_chunks=n_chunks, ctx_chunks=ctx_len // tc, nb=b),
        grid=(n_slabs,),
        in_specs=[
            pl.BlockSpec((1, rows, tc * slab), blk),
            _const_spec(perm.shape),
            pl.BlockSpec((n_sub, w, w), blk),
            pl.BlockSpec((n_sub, w, 4 * S5_STATE), blk),
            pl.BlockSpec((n_sub, 4 * S5_STATE, w), blk),
            pl.BlockSpec((n_sub, 8, 2 * S5_STATE), blk),
        ],
        out_specs=pl.BlockSpec((1, rows, tc * slab), blk),
        out_shape=jax.ShapeDtypeStruct((n_slabs, rows, tc * slab), BF16),
        scratch_shapes=[pltpu.VMEM((rows, 4 * S5_STATE), F32), pltpu.VMEM((rows, 4 * S5_STATE), F32),
                        pltpu.VMEM((rows, tc * slab), BF16)],
        compiler_params=_params("parallel"),
        name="s5_core",
    )(u, perm, m, p, q, dec)
    return y.reshape(n_slabs, n_chunks, b, tc, slab).transpose(2, 1, 3, 0, 4).reshape(b, s, d)


def _glu_res_kernel(y_ref, x_ref, md_ref, g_ref, w_ref, b_ref, o_ref):
    y = y_ref[0].astype(F32)
    gel = y * (0.5 * (1.0 + jnp.tanh(np.sqrt(2.0 / np.pi).astype(np.float32) * (y + 0.044715 * (y * y * y)))))
    out = gel * _sigmoid(_dot(gel.astype(BF16), w_ref[...]) + b_ref[...])
    o_ref[0] = x_ref[0] + md_ref[0, 0][2:3] * _rms(out, g_ref[1:2])


def _glu_res(y, xs, md, g, w, bias, ctx_tiles=1):
    b, s, d = xs.shape
    return pl.pallas_call(
        _glu_res_kernel,
        grid=(b, s // ROW_TILE),
        in_specs=[_row_spec(d), _row_spec(d), _mod_spec(d, ctx_tiles), _const_spec(g.shape),
                  _const_spec(w.shape), _const_spec((1, d))],
        out_specs=_row_spec(d),
        out_shape=jax.ShapeDtypeStruct((b, s, d), F32),
        compiler_params=_params("parallel", "parallel"),
        name="glu_res",
    )(y, xs, md, g, w, bias.reshape(1, d))


def _conv_silu_kernel(x_ref, prev_ref, next_ref, w_ref, b_ref, o_ref, *, ctx_len, seq_len, width):
    t = pl.program_id(1)
    tm = x_ref.shape[1]
    halo = prev_ref.shape[1]
    x = x_ref[0].astype(F32)
    lo = t * tm
    hi = lo + tm
    prev_ok = jnp.where((lo == 0) | (lo == ctx_len), 0.0, 1.0)
    next_ok = jnp.where((hi == ctx_len) | (hi == seq_len), 0.0, 1.0)
    prev = prev_ref[0].astype(F32) * prev_ok
    nxt = next_ref[0].astype(F32) * next_ok
    row = lax.broadcasted_iota(jnp.int32, (8, x.shape[1]), 0)
    prev_tail = prev[halo - 8:halo]
    next_head = nxt[0:8]
    half = width // 2
    acc = x * w_ref[half:half + 1] + b_ref[...]
    xb = x_ref[0]
    lag = lax.broadcasted_iota(jnp.int32, (tm, tm), 0) - lax.broadcasted_iota(jnp.int32, (tm, tm), 1)
    for k in range(1, half + 1):
        back = _dot(jnp.where(lag == k, 1.0, 0.0).astype(BF16), xb)
        top = jnp.where(row < k, pltpu.roll(prev_tail, k, 0), back[0:8])
        back = jnp.concatenate([top, back[8:]], axis=0)
        acc = acc + back * w_ref[half - k:half - k + 1]
        fwd = _dot(jnp.where(lag == -k, 1.0, 0.0).astype(BF16), xb)
        bottom = jnp.where(row >= 8 - k, pltpu.roll(next_head, 8 - k, 0), fwd[tm - 8:tm])
        fwd = jnp.concatenate([fwd[:tm - 8], bottom], axis=0)
        acc = acc + fwd * w_ref[half + k:half + k + 1]
    o_ref[0] = (acc * _sigmoid(acc)).astype(o_ref.dtype)


def _conv_silu(x, w, bias, ctx_len, width):
    assert x.dtype == BF16
    b, s, n = x.shape
    tm = ROW_TILE
    halo = 16
    sub = tm // halo
    last = s // halo - 1
    return pl.pallas_call(
        functools.partial(_conv_silu_kernel, ctx_len=ctx_len, seq_len=s, width=width),
        grid=(b, s // tm),
        in_specs=[
            _row_spec(n),
            pl.BlockSpec((1, halo, n), lambda i, t: (i, jnp.maximum(t * sub - 1, 0), 0)),
            pl.BlockSpec((1, halo, n), lambda i, t: (i, jnp.minimum((t + 1) * sub, last), 0)),
            _const_spec(w.shape),
            _const_spec((1, n)),
        ],
        out_specs=_row_spec(n),
        out_shape=jax.ShapeDtypeStruct((b, s, n), BF16),
        compiler_params=_params("parallel", "parallel"),
        name="conv_silu",
    )(x, x, x, w, bias.reshape(1, n))


def _softplus(z):
    return jnp.maximum(z, 0.0) + jnp.log1p(jnp.exp(-jnp.abs(z)))


def _expand_heads(v, e3):
    hi = v.astype(BF16)
    r1 = v - hi.astype(F32)
    mid = r1.astype(BF16)
    lo = (r1 - mid.astype(F32)).astype(BF16)
    return _dot(jnp.concatenate([hi, mid, lo], axis=1), e3)


def _ssd_kernel(*refs, rev, n_heads):
    if rev:
        (xs_ref, bm_ref, cm_ref, dtc_ref, dtr_ref, pr_ref, pc_ref, yprev_ref, dsk_ref, y_ref, st_ref) = refs
    else:
        (xs_ref, bm_ref, cm_ref, dtc_ref, dtr_ref, pr_ref, pc_ref, y_ref, st_ref) = refs
    t = pl.program_id(1)
    tc = xs_ref.shape[1]
    hd = SSD_HEAD_DIM
    pair = 2 * hd
    heads_per_group = n_heads // SSD_GROUPS

    @pl.when(t == 0)
    def _():
        st_ref[...] = jnp.zeros(st_ref.shape, F32)

    dtc = _softplus(dtc_ref[0, 0] + pr_ref[0:1])
    dtr = _softplus(dtr_ref[0, 0] + pc_ref[:, 0:1])
    dac = dtc * -jnp.exp(pr_ref[1:2])
    dar = dtr * -jnp.exp(pc_ref[:, 1:2])
    ii = lax.broadcasted_iota(jnp.int32, (tc, tc), 0)
    jj = lax.broadcasted_iota(jnp.int32, (tc, tc), 1)
    visible = (jj >= ii) if rev else (ii >= jj)
    tri = jnp.where(visible, 1.0, 0.0)
    hp = lax.Precision.HIGHEST
    cum_c = jnp.dot(tri, dac, precision=hp, preferred_element_type=F32)
    cum_r = lax.dot_general(dar, tri, (((1,), (1,)), ((), ())), precision=hp, preferred_element_type=F32)
    tot = cum_c[0:1] if rev else cum_c[tc - 1:tc]

    e_row = lax.broadcasted_iota(jnp.int32, (3 * n_heads, n_heads * hd), 0)
    e_col = lax.broadcasted_iota(jnp.int32, (3 * n_heads, n_heads * hd), 1)
    e3 = jnp.where((e_row % n_heads) == (e_col // hd), 1.0, 0.0).astype(BF16)
    w_out = _expand_heads(dtc * jnp.exp(tot - cum_c), e3)
    w_in = _expand_heads(jnp.exp(cum_c), e3)
    dec = _expand_heads(jnp.broadcast_to(jnp.exp(tot), (8, n_heads)), e3)[0:1]

    xs = xs_ref[0]
    lane = lax.broadcasted_iota(jnp.int32, (tc, pair), 1)
    first = lane < hd
    for g in range(SSD_GROUPS):
        bmb = bm_ref[0, :, g * SSD_STATE:(g + 1) * SSD_STATE].astype(BF16)
        cmb = cm_ref[0, :, g * SSD_STATE:(g + 1) * SSD_STATE].astype(BF16)
        scores = _dot_nt(cmb, bmb)
        for k in range(heads_per_group // 2):
            hp_i = g * (heads_per_group // 2) + k
            cols = slice(hp_i * pair, (hp_i + 1) * pair)
            x2b = xs[:, cols]
            x2 = x2b.astype(F32)
            halves = []
            for s_i in range(2):
                h = 2 * hp_i + s_i
                decay = jnp.exp(jnp.where(visible, cum_c[:, h:h + 1] - cum_r[h:h + 1, :], -jnp.inf))
                wgt = (scores * decay * dtr[h:h + 1, :]).astype(BF16)
                halves.append(_dot(wgt, x2b))
            y_diag = jnp.where(first, halves[0], halves[1])
            st = st_ref[hp_i]
            y_off = _dot(cmb, st.astype(BF16)) * w_in[:, cols]
            y = y_diag + y_off
            if rev:
                y = y + yprev_ref[0, :, cols].astype(F32) + x2 * dsk_ref[:, cols]
            y_ref[0, :, cols] = y.astype(y_ref.dtype)
            st_ref[hp_i] = dec[:, cols] * st + _dot_tn(bmb, (x2 * w_out[:, cols]).astype(BF16))


def _ssd_scan(xbc, dt_cols, dt_rows, prm_rows, prm_cols, d_inner, ctx_len, rev, y_prev=None, d_skip=None):
    b, s, _ = xbc.shape
    tc = SSD_CHUNK
    n_heads = d_inner // SSD_HEAD_DIM
    nbc = SSD_GROUPS * SSD_STATE
    n_chunks = s // tc
    ctx_chunks = ctx_len // tc
    d = 1 if rev else 0
    if rev:
        chunk = lambda t: jnp.where(t < ctx_chunks, ctx_chunks - 1 - t, n_chunks - 1 - (t - ctx_chunks))
    else:
        chunk = lambda t: t
    in_specs = [
        pl.BlockSpec((1, tc, d_inner), lambda i, t: (i, chunk(t), 0)),
        pl.BlockSpec((1, tc, nbc), lambda i, t: (i, chunk(t), d_inner // nbc)),
        pl.BlockSpec((1, tc, nbc), lambda i, t: (i, chunk(t), d_inner // nbc + 1)),
        pl.BlockSpec((1, 1, tc, n_heads), lambda i, t: (i, d, chunk(t), 0)),
        pl.BlockSpec((1, 1, n_heads, tc), lambda i, t: (i, d, 0, chunk(t))),
        _const_spec(prm_rows.shape),
        _const_spec(prm_cols.shape),
    ]
    args = [xbc, xbc, xbc, dt_cols, dt_rows, prm_rows, prm_cols]
    if rev:
        in_specs += [pl.BlockSpec((1, tc, d_inner), lambda i, t: (i, chunk(t), 0)), _const_spec((1, d_inner))]
        args += [y_prev, d_skip]
    return pl.pallas_call(
        functools.partial(_ssd_kernel, rev=rev, n_heads=n_heads),
        grid=(b, n_chunks),
        in_specs=in_specs,
        out_specs=pl.BlockSpec((1, tc, d_inner), lambda i, t: (i, chunk(t), 0)),
        out_shape=jax.ShapeDtypeStruct((b, s, d_inner), BF16),
        scratch_shapes=[pltpu.VMEM((n_heads // 2, SSD_STATE, 2 * SSD_HEAD_DIM), F32)],
        compiler_params=_params("parallel", "arbitrary"),
        name="ssd_rev" if rev else "ssd_fwd",
    )(*args)


def _ssd_out_kernel(y_ref, gt_ref, x_ref, md_ref, g_ref, ng_ref, w_ref, o_ref):
    gt = gt_ref[0].astype(F32)
    yg = y_ref[0].astype(F32) * (gt * _sigmoid(gt))
    width = yg.shape[1] // SSD_GROUPS
    r = jnp.zeros(x_ref.shape[1:], F32)
    for k in range(SSD_GROUPS):
        seg = _rms(yg[:, k * width:(k + 1) * width], ng_ref[:, k * width:(k + 1) * width])
        r = r + _dot(seg.astype(BF16), w_ref[k * width:(k + 1) * width, :])
    o_ref[0] = x_ref[0] + md_ref[0, 0][2:3] * _rms(r, g_ref[1:2])


def _ssd_out(y, gate, xs, md, g, norm_g, w, ctx_tiles=1):
    b, s, d = xs.shape
    n = y.shape[-1]
    return pl.pallas_call(
        _ssd_out_kernel,
        grid=(b, s // ROW_TILE),
        in_specs=[_row_spec(n), _row_spec(n), _row_spec(d), _mod_spec(d, ctx_tiles), _const_spec(g.shape),
                  _const_spec((1, n)), _const_spec(w.shape)],
        out_specs=_row_spec(d),
        out_shape=jax.ShapeDtypeStruct((b, s, d), F32),
        compiler_params=_params("parallel", "parallel"),
        name="ssd_out",
    )(y, gate, xs, md, g, norm_g.reshape(1, n), w)


def _na_kernel(q_ref, k_ref, v_ref, bias_ref, o_ref, *, ctx_len, grid_rows):
    r = pl.program_id(1)
    hd = NA_HEAD_DIM
    pair = 2 * hd
    n_heads = q_ref.shape[2] // hd
    win = NA_WIN_ROWS * GRID_W
    r0 = jnp.clip(r - NA_WIN_ROWS // 2, 0, grid_rows - NA_WIN_ROWS)
    base = pl.multiple_of(ctx_len + r0 * GRID_W, GRID_W)
    lane = lax.broadcasted_iota(jnp.int32, (GRID_W, pair), 1)
    first = lane < hd
    scale = hd ** -0.5
    s_loc, s_ctx = [], []
    for h in range(n_heads):
        cols = slice((h // 2) * pair, (h // 2 + 1) * pair)
        qm = jnp.where(first if h % 2 == 0 else ~first, q_ref[0, :, cols] * scale, 0.0).astype(BF16)
        s_loc.append(_dot_nt(qm, k_ref[0, pl.ds(base, win), cols]) + bias_ref[0, h])
        s_ctx.append(_dot_nt(qm, k_ref[0, 0:ctx_len, cols]))
    p_loc, p_ctx, inv = [], [], []
    for h in range(n_heads):
        m = jnp.maximum(jnp.max(s_loc[h], axis=-1, keepdims=True), jnp.max(s_ctx[h], axis=-1, keepdims=True))
        e_loc = jnp.exp(s_loc[h] - m)
        e_ctx = jnp.exp(s_ctx[h] - m)
        inv.append(1.0 / (jnp.sum(e_loc, axis=-1, keepdims=True) + jnp.sum(e_ctx, axis=-1, keepdims=True)))
        p_loc.append(e_loc.astype(BF16))
        p_ctx.append(e_ctx.astype(BF16))
    for hp_i in range(n_heads // 2):
        cols = slice(hp_i * pair, (hp_i + 1) * pair)
        vw = v_ref[0, pl.ds(base, win), cols]
        vc = v_ref[0, 0:ctx_len, cols]
        halves = [(_dot(p_loc[h], vw) + _dot(p_ctx[h], vc)) * inv[h] for h in (2 * hp_i, 2 * hp_i + 1)]
        o_ref[0, :, cols] = jnp.where(first, halves[0], halves[1]).astype(o_ref.dtype)


def _na_bias_table(rpb):
    n_heads, n_rr, n_cr = rpb.shape
    qc = np.arange(GRID_W)[:, None]
    kc = np.arange(GRID_W)[None, :]
    start = np.clip(qc - NA_WIN_COLS // 2, 0, GRID_W - NA_WIN_COLS)
    inside = (kc >= start) & (kc < start + NA_WIN_COLS)
    col_rel = kc - qc + (NA_WIN_COLS - 1)
    onehot = ((col_rel[None] == np.arange(n_cr)[:, None, None]) & inside[None]).astype(np.float32)
    t2 = jnp.dot(rpb.astype(F32).reshape(n_heads * n_rr, n_cr), jnp.asarray(onehot.reshape(n_cr, GRID_W * GRID_W)),
                 precision=lax.Precision.HIGHEST).reshape(n_heads, n_rr, GRID_W, GRID_W)
    t2 = jnp.where(jnp.asarray(inside), t2, NEG_BIG)
    tbl = jnp.stack([t2[:, NA_WIN_ROWS - 1 - dr:2 * NA_WIN_ROWS - 1 - dr] for dr in range(NA_WIN_ROWS)])
    return jnp.transpose(tbl, (0, 1, 3, 2, 4)).reshape(NA_WIN_ROWS, n_heads, GRID_W, NA_WIN_ROWS * GRID_W)


def _na_attention(q, k, v, bias, ctx_len):
    b, s, d = q.shape
    n_lat = s - ctx_len
    grid_rows = n_lat // GRID_W
    n_heads = d // NA_HEAD_DIM
    q_off = ctx_len // GRID_W

    def bias_index(i, r):
        return (r - jnp.clip(r - NA_WIN_ROWS // 2, 0, grid_rows - NA_WIN_ROWS), 0, 0, 0)

    return pl.pallas_call(
        functools.partial(_na_kernel, ctx_len=ctx_len, grid_rows=grid_rows),
        grid=(b, grid_rows),
        in_specs=[
            pl.BlockSpec((1, GRID_W, d), lambda i, r: (i, q_off + r, 0)),
            pl.BlockSpec((1, s, d), lambda i, r: (i, 0, 0)),
            pl.BlockSpec((1, s, d), lambda i, r: (i, 0, 0)),
            pl.BlockSpec((1, n_heads, GRID_W, NA_WIN_ROWS * GRID_W), bias_index),
        ],
        out_specs=pl.BlockSpec((1, GRID_W, d), lambda i, r: (i, r, 0)),
        out_shape=jax.ShapeDtypeStruct((b, n_lat, d), BF16),
        compiler_params=_params("parallel", "arbitrary"),
        name="na_attention",
    )(q, k, v, bias)


def _forward(x, c, ctx, c_ctx, ada_w, ada_b, norm_g, mlp_w1, mlp_w2,
             gqa_w_qkv, gqa_q_norm, gqa_k_norm, gqa_w_o,
             s5_a_re, s5_a_im, s5_log_dt, s5_b_re, s5_b_im, s5_c_re, s5_c_im, s5_d, s5_glu_w, s5_glu_b,
             ssd_w_in, ssd_conv_w, ssd_conv_b, ssd_dt_bias, ssd_a_log, ssd_d, ssd_norm_g, ssd_w_out,
             na_w_qkv, na_rpb, na_w_o):
    bsz, n_lat, d = x.shape
    ctx_len = ctx.shape[1]
    depth = ada_w.shape[0]
    n_mod = ada_w.shape[2] // d
    assert depth == 4 and ctx_len == ROW_TILE and n_lat % ROW_TILE == 0 and n_lat % GRID_W == 0
    assert n_lat // GRID_W >= NA_WIN_ROWS

    cond = jnp.concatenate([c, c_ctx[None]], axis=0)
    cond = jnp.pad(cond, ((0, -cond.shape[0] % 8), (0, 0)))
    ada = _ada(cond, ada_w, ada_b)

    xs = jnp.concatenate([ctx, x], axis=1)
    streams = []
    for i in range(depth):
        kind = i % 4
        j = i // 4
        mod = ada[i, :bsz].reshape(bsz, n_mod, d)
        modc = jnp.broadcast_to(ada[i, bsz].reshape(1, n_mod, d), (bsz, n_mod, d))
        md = jnp.pad(jnp.stack([modc, mod], axis=1), ((0, 0), (0, 0), (0, 8 - n_mod), (0, 0)))
        g = jnp.pad(norm_g[i], ((0, 4), (0, 0)))
        w1 = mlp_w1[i].astype(BF16)
        w2 = mlp_w2[i].astype(BF16)
        ctx_tiles = 1
        if kind == 0:
            w_qkv = gqa_w_qkv[j].astype(BF16)
            n = w_qkv.shape[1]
            (qkv,) = _nm_matmul(xs, md, g, w_qkv, ((0, n),), (F32,))
            o = _gqa_attention(qkv, gqa_q_norm[j], gqa_k_norm[j], ctx_len)
            xs = _proj_res(o, xs, md, g, gqa_w_o[j].astype(BF16))
        elif kind == 1:
            h = _nm(xs, md, g, BF16)
            tables = _s5_tables(s5_a_re[j], s5_a_im[j], s5_log_dt[j], s5_b_re[j], s5_b_im[j],
                                s5_c_re[j], s5_c_im[j], s5_d[j])
            y = _s5_core(h, tables, ctx_len)
            xs = _glu_res(y, xs, md, g, s5_glu_w[j].astype(BF16), s5_glu_b[j])
        elif kind == 2:
            w_in = ssd_w_in[j].astype(BF16)
            d_inner = ssd_w_out.shape[1]
            n_heads = d_inner // SSD_HEAD_DIM
            conv_ch = ssd_conv_w.shape[2]
            splits = ((0, d_inner), (d_inner, d_inner + conv_ch), (d_inner + conv_ch, w_in.shape[1]))
            gate, pre, dt_raw = _nm_matmul(xs, md, g, w_in, splits, (BF16, BF16, F32))
            conv_w = jnp.pad(ssd_conv_w[j], ((0, 8 - ssd_conv_w.shape[1]), (0, 0)))
            xbc = _conv_silu(pre, conv_w, ssd_conv_b[j], ctx_len, ssd_conv_w.shape[1])
            dt_cols = dt_raw.reshape(bsz, -1, 2, n_heads).transpose(0, 2, 1, 3)
            dt_rows = dt_cols.transpose(0, 1, 3, 2)
            d_skip = jnp.repeat(ssd_d[j].astype(F32), SSD_HEAD_DIM).reshape(1, d_inner)
            y = None
            for direction in (0, 1):
                prm = jnp.stack([ssd_dt_bias[j, direction], ssd_a_log[j, direction]]).astype(F32)
                prm_rows = jnp.pad(prm, ((0, 6), (0, 0)))
                prm_cols = jnp.pad(prm.T, ((0, 0), (0, 6)))
                y = _ssd_scan(xbc, dt_cols, dt_rows, prm_rows, prm_cols, d_inner, ctx_len, direction == 1,
                              y_prev=y, d_skip=d_skip)
            xs = _ssd_out(y, gate, xs, md, g, ssd_norm_g[j], ssd_w_out[j].astype(BF16))
        else:
            w_qkv = na_w_qkv[j].astype(BF16)
            q, k, v = _nm_matmul(xs, md, g, w_qkv, ((0, d), (d, 2 * d), (2 * d, 3 * d)), (F32, BF16, BF16))
            o = _na_attention(q, k, v, _na_bias_table(na_rpb[j]), ctx_len)
            ctx_tiles = 0
            xs = _proj_res(o, xs, md, g, na_w_o[j].astype(BF16), ctx_tiles=ctx_tiles,
                           skip_tiles=ctx_len // ROW_TILE)
        xs = _mlp(xs, md, g, w1, w2, ctx_tiles=ctx_tiles)
        streams.append(xs)
    return streams


def kernel(x, c, ctx, c_ctx, ada_w, ada_b, norm_g, mlp_w1, mlp_w2, gqa_w_qkv, gqa_q_norm, gqa_k_norm, gqa_w_o, s5_a_re, s5_a_im, s5_log_dt, s5_b_re, s5_b_im, s5_c_re, s5_c_im, s5_d, s5_glu_w, s5_glu_b, ssd_w_in, ssd_conv_w, ssd_conv_b, ssd_dt_bias, ssd_a_log, ssd_d, ssd_norm_g, ssd_w_out, na_w_qkv, na_rpb, na_w_o):
    return _forward(x, c, ctx, c_ctx, ada_w, ada_b, norm_g, mlp_w1, mlp_w2, gqa_w_qkv, gqa_q_norm, gqa_k_norm,
                    gqa_w_o, s5_a_re, s5_a_im, s5_log_dt, s5_b_re, s5_b_im, s5_c_re, s5_c_im, s5_d, s5_glu_w,
                    s5_glu_b, ssd_w_in, ssd_conv_w, ssd_conv_b, ssd_dt_bias, ssd_a_log, ssd_d, ssd_norm_g,
                    ssd_w_out, na_w_qkv, na_rpb, na_w_o)[-1]
```
